```python
import jax, jax.numpy as jnp
from jax import lax
import numpy as np

D_MODEL = 1024
BATCH = 8
SEQ = 8192
DEPTH = 2

N_A = DEPTH // 2
N_B = DEPTH - N_A
N_HEADS = 16
HEAD_DIM = D_MODEL // N_HEADS
D_FF = 2816
POOL_WINDOWS = (2, 4, 8, 16)
N_POOL_GROUPS = len(POOL_WINDOWS)
GROUP_W = D_MODEL // N_POOL_GROUPS
Q_BLOCK = 128
RMS_EPS = 1e-6
FFN_RES_WEIGHT = 0.5

kernel_name = "yoco_pool_stickbreak_macaron"


def rms_norm(x, g):
    xf = x.astype(jnp.float32)
    y = xf * lax.rsqrt(jnp.mean(xf * xf, axis=-1, keepdims=True) + RMS_EPS)
    return (y * g.astype(jnp.float32)).astype(x.dtype)


def swiglu(x, w_in, w_out):
    gate, up = jnp.split(x @ w_in, 2, axis=-1)
    return (jax.nn.silu(gate) * up) @ w_out


def pool_mixer(x, w_groups, scale):
    B, S, D = x.shape
    xf = x.astype(jnp.float32)
    cs = jnp.cumsum(xf, axis=1)
    pos = jnp.arange(S)
    outs = []
    for g, w in enumerate(POOL_WINDOWS):
        sl = slice(g * GROUP_W, (g + 1) * GROUP_W)
        c = cs[..., sl]
        prev = jnp.pad(c, ((0, 0), (w, 0), (0, 0)))[:, :S]
        cnt = jnp.minimum(pos + 1, w).astype(jnp.float32)[None, :, None]
        outs.append((c - prev) / cnt - xf[..., sl])
    p = jnp.stack(outs, axis=2).astype(x.dtype)
    y = jnp.einsum('bsgc,gcd->bsgd', p, w_groups).reshape(B, S, D)
    return y * scale


def stick_breaking_attention(q, k, v):
    B, H, S, hd = q.shape
    nb = S // Q_BLOCK
    scale = hd ** -0.5
    q_blocks = q.reshape(B, H, nb, Q_BLOCK, hd).transpose(2, 0, 1, 3, 4)
    idx = jnp.arange(Q_BLOCK)

    def per_query_block(args):
        i, qi = args
        qi = qi.astype(jnp.float32) * scale
        q_pos = i * Q_BLOCK + idx

        def body(step, carry):
            o, surv = carry
            j = i - step
            kj = lax.dynamic_slice_in_dim(k, j * Q_BLOCK, Q_BLOCK, axis=2).astype(jnp.float32)
            vj = lax.dynamic_slice_in_dim(v, j * Q_BLOCK, Q_BLOCK, axis=2).astype(jnp.float32)
            z = jnp.einsum('bhqd,bhkd->bhqk', qi, kj)
            k_pos = j * Q_BLOCK + idx
            mask = k_pos[None, :] < q_pos[:, None]
            lneg = jnp.where(mask, jax.nn.log_sigmoid(-z), 0.0)
            csum = jnp.cumsum(lneg, axis=-1)
            row = csum[..., -1]
            suffix = row[..., None] - csum + surv[..., None]
            a = jnp.where(mask, jnp.exp(jax.nn.log_sigmoid(z) + suffix), 0.0)
            o = o + jnp.einsum('bhqk,bhkd->bhqd', a, vj)
            return o, surv + row

        o0 = jnp.zeros((B, H, Q_BLOCK, hd), jnp.float32)
        s0 = jnp.zeros((B, H, Q_BLOCK), jnp.float32)
        o, _ = lax.fori_loop(0, i + 1, body, (o0, s0))
        return o.astype(v.dtype)

    out = lax.map(per_query_block, (jnp.arange(nb), q_blocks))
    return out.transpose(1, 2, 0, 3, 4).reshape(B, H, S, hd)


def setup_inputs(seed: int = 0) -> dict:
    key = jax.random.key(seed)
    ks = jax.random.split(key, 20)
    f32 = jnp.float32
    D, F = D_MODEL, D_FF

    def nrm(k, shape, fan_in):
        return jax.random.normal(k, shape, f32) * (fan_in ** -0.5)

    def gain(k, shape):
        return 1.0 + 0.02 * jax.random.normal(k, shape, f32)

    return {
        "x": jax.random.normal(ks[0], (BATCH, SEQ, D), f32),
        "ffn1_norm": gain(ks[1], (DEPTH, D)),
        "ffn1_w_in": nrm(ks[2], (DEPTH, D, 2 * F), D),
        "ffn1_w_out": nrm(ks[3], (DEPTH, F, D), F),
        "ffn2_norm": gain(ks[4], (DEPTH, D)),
        "ffn2_w_in": nrm(ks[5], (DEPTH, D, 2 * F), D),
        "ffn2_w_out": nrm(ks[6], (DEPTH, F, D), F),
        "pool_norm": gain(ks[7], (N_A, D)),
        "pool_w": nrm(ks[8], (N_A, N_POOL_GROUPS, GROUP_W, GROUP_W), GROUP_W),
        "pool_scale": gain(ks[9], (N_A, D)),
        "kv_norm": gain(ks[10], (D,)),
        "w_kv": nrm(ks[11], (D, 2 * D), D),
        "k_gain": gain(ks[12], (HEAD_DIM,)),
        "attn_norm": gain(ks[13], (N_B, D)),
        "w_q": nrm(ks[14], (N_B, D, D), D),
        "q_gain": gain(ks[15], (N_B, HEAD_DIM)),
        "w_o": nrm(ks[16], (N_B, D, D), D),
    }


def reference(x, ffn1_norm, ffn1_w_in, ffn1_w_out, ffn2_norm, ffn2_w_in, ffn2_w_out,
              pool_norm, pool_w, pool_scale, kv_norm, w_kv, k_gain,
              attn_norm, w_q, q_gain, w_o):
    B, S, D = x.shape
    h = x
    k = None
    v = None
    for l in range(DEPTH):
        h = h + FFN_RES_WEIGHT * swiglu(rms_norm(h, ffn1_norm[l]), ffn1_w_in[l], ffn1_w_out[l])
        if l < N_A:
            h = h + pool_mixer(rms_norm(h, pool_norm[l]), pool_w[l], pool_scale[l])
        else:
            b = l - N_A
            q = (rms_norm(h, attn_norm[b]) @ w_q[b]).reshape(B, S, N_HEADS, HEAD_DIM)
            q = rms_norm(q, q_gain[b]).transpose(0, 2, 1, 3)
            o = stick_breaking_attention(q, k, v)
            h = h + o.transpose(0, 2, 1, 3).reshape(B, S, D) @ w_o[b]
        h = h + FFN_RES_WEIGHT * swiglu(rms_norm(h, ffn2_norm[l]), ffn2_w_in[l], ffn2_w_out[l])
        if l == N_A - 1:
            k_flat, v_flat = jnp.split(rms_norm(h, kv_norm) @ w_kv, 2, axis=-1)
            k = rms_norm(k_flat.reshape(B, S, N_HEADS, HEAD_DIM), k_gain).transpose(0, 2, 1, 3)
            v = v_flat.reshape(B, S, N_HEADS, HEAD_DIM).transpose(0, 2, 1, 3)
    return h
```

```python
import functools

import jax
import jax.numpy as jnp
from jax import lax
from jax.experimental import pallas as pl
from jax.experimental.pallas import tpu as pltpu

N_HEADS = 16
HEAD_DIM = 64
POOL_WINDOWS = (2, 4, 8, 16)
MAX_WINDOW = max(POOL_WINDOWS)
Q_BLOCK = 128
RMS_EPS = 1e-6
FFN_RES_WEIGHT = 0.5

LANES = 128
MXU_DIM = 256
VMEM_LIMIT_BYTES = 56 * 1024 * 1024

FFN_ROWS = 512
FFN_CHUNK = MXU_DIM
PROJ_ROWS = 512
POOL_ROWS = 512
HEADS_PER_STEP = LANES // HEAD_DIM

LOG_SURVIVAL_FLOOR = -105.0

_BF16 = jnp.bfloat16
_F32 = jnp.float32


def _resident(shape):
    zeros = (0,) * len(shape)
    return pl.BlockSpec(shape, lambda *_: zeros, pipeline_mode=pl.Buffered(1))


def _rms_norm(x, gain):
    ms = jnp.mean(x * x, axis=-1, keepdims=True)
    return x * lax.rsqrt(ms + RMS_EPS) * gain


def _dot(a, b):
    return jnp.dot(a, b, preferred_element_type=_F32)


def _split_bf16(x):
    hi = x.astype(_BF16)
    lo = (x - hi.astype(_F32)).astype(_BF16)
    return hi, lo


def _ffn_kernel(x_ref, g_ref, win_ref, wout_ref, o_ref, *, d_ff):
    x = x_ref[...]
    xn = _rms_norm(x, g_ref[...]).astype(_BF16)
    acc = jnp.zeros(x.shape, _F32)
    for c in range(d_ff // FFN_CHUNK):
        lo = c * FFN_CHUNK
        gate = _dot(xn, win_ref[:, lo:lo + FFN_CHUNK])
        up = _dot(xn, win_ref[:, d_ff + lo:d_ff + lo + FFN_CHUNK])
        act = (gate * jax.nn.sigmoid(gate) * up).astype(_BF16)
        acc = acc + _dot(act, wout_ref[lo:lo + FFN_CHUNK, :])
    o_ref[...] = x + FFN_RES_WEIGHT * acc


def _ffn(h, gain, w_in, w_out):
    t, d = h.shape
    d_ff = w_out.shape[0]
    assert t % FFN_ROWS == 0 and d_ff % FFN_CHUNK == 0
    row_spec = pl.BlockSpec((FFN_ROWS, d), lambda i: (i, 0))
    return pl.pallas_call(
        functools.partial(_ffn_kernel, d_ff=d_ff),
        grid=(t // FFN_ROWS,),
        in_specs=[row_spec, _resident((1, d)), _resident(w_in.shape), _resident(w_out.shape)],
        out_specs=row_spec,
        out_shape=jax.ShapeDtypeStruct(h.shape, h.dtype),
        compiler_params=pltpu.CompilerParams(
            dimension_semantics=("arbitrary",), vmem_limit_bytes=VMEM_LIMIT_BYTES),
        name="ffn",
    )(h, gain.reshape(1, d), w_in.astype(_BF16), w_out.astype(_BF16))


def _pool_kernel(x_ref, halo_ref, g_ref, w_ref, scale_ref, o_ref):
    i = pl.program_id(1)
    x = x_ref[0]
    gain = g_ref[...]
    xn = _rms_norm(x, gain)
    halo = jnp.where(i > 0, _rms_norm(halo_ref[0], gain), 0.0)
    ext = jnp.concatenate([halo, xn], axis=0)
    rows = x.shape[0]
    pos = i * rows + lax.broadcasted_iota(jnp.int32, (rows, 1), 0)
    group_w = w_ref.shape[1]
    outs = []
    for g, w in enumerate(POOL_WINDOWS):
        cols = slice(g * group_w, (g + 1) * group_w)
        s = ext[:, cols]
        shift = 1
        while shift < w:
            s = s + pltpu.roll(s, shift, axis=0)
            shift *= 2
        inv_cnt = 1.0 / jnp.minimum(pos + 1, w).astype(_F32)
        p = s[MAX_WINDOW:] * inv_cnt - xn[:, cols]
        outs.append(_dot(p.astype(_BF16), w_ref[g]))
    y = jnp.concatenate(outs, axis=-1)
    o_ref[0] = x + y * scale_ref[...]


def _pool(h3, gain, w_groups, scale):
    b, s, d = h3.shape
    assert s % POOL_ROWS == 0 and POOL_ROWS % MAX_WINDOW == 0
    halo_blocks = POOL_ROWS // MAX_WINDOW
    row_spec = pl.BlockSpec((1, POOL_ROWS, d), lambda bi, i: (bi, i, 0))
    halo_spec = pl.BlockSpec(
        (1, MAX_WINDOW, d), lambda bi, i: (bi, jnp.maximum(i * halo_blocks - 1, 0), 0))
    return pl.pallas_call(
        _pool_kernel,
        grid=(b, s // POOL_ROWS),
        in_specs=[row_spec, halo_spec, _resident((1, d)), _resident(w_groups.shape),
                  _resident((1, d))],
        out_specs=row_spec,
        out_shape=jax.ShapeDtypeStruct(h3.shape, h3.dtype),
        compiler_params=pltpu.CompilerParams(
            dimension_semantics=("arbitrary", "arbitrary"), vmem_limit_bytes=VMEM_LIMIT_BYTES),
        name="pool",
    )(h3, h3, gain.reshape(1, d), w_groups.astype(_BF16), scale.reshape(1, d))


def _head_sum_matrices(d):
    head_of_col = jnp.arange(d) // HEAD_DIM
    gather = (head_of_col[:, None] == jnp.arange(LANES)[None, :]).astype(_BF16)
    return gather, gather.T


def _head_proj_kernel(x_ref, g_ref, w_ref, hg_ref, gather_ref, spread_ref, *o_refs, d_norm):
    xn = _rms_norm(x_ref[...], g_ref[...]).astype(_BF16)
    y = _dot(xn, w_ref[...])
    yn = y[:, :d_norm]
    sq_hi, sq_lo = _split_bf16(yn * yn)
    head_ss = _dot(sq_hi, gather_ref[...]) + _dot(sq_lo, gather_ref[...])
    r = lax.rsqrt(head_ss * (1.0 / HEAD_DIM) + RMS_EPS)
    r_hi, r_lo = _split_bf16(r)
    r_full = _dot(r_hi, spread_ref[...]) + _dot(r_lo, spread_ref[...])
    o_refs[0][...] = (yn * r_full * hg_ref[...]).astype(o_refs[0].dtype)
    if len(o_refs) > 1:
        o_refs[1][...] = y[:, d_norm:].astype(o_refs[1].dtype)


def _head_proj(h, gain, w, head_gain, d_norm):
    t, d = h.shape
    d_out = w.shape[1]
    assert t % PROJ_ROWS == 0 and d_norm % HEAD_DIM == 0 and d_norm // HEAD_DIM <= LANES
    gather, spread = _head_sum_matrices(d_norm)
    hg = jnp.tile(head_gain, d_norm // HEAD_DIM).reshape(1, d_norm)
    row_spec = pl.BlockSpec((PROJ_ROWS, d), lambda i: (i, 0))
    out_shapes = [jax.ShapeDtypeStruct((t, d_norm), _BF16)]
    out_specs = [pl.BlockSpec((PROJ_ROWS, d_norm), lambda i: (i, 0))]
    if d_out > d_norm:
        out_shapes.append(jax.ShapeDtypeStruct((t, d_out - d_norm), _BF16))
        out_specs.append(pl.BlockSpec((PROJ_ROWS, d_out - d_norm), lambda i: (i, 0)))
    return pl.pallas_call(
        functools.partial(_head_proj_kernel, d_norm=d_norm),
        grid=(t // PROJ_ROWS,),
        in_specs=[row_spec, _resident((1, d)), _resident(w.shape), _resident((1, d_norm)),
                  _resident(gather.shape), _resident(spread.shape)],
        out_specs=out_specs,
        out_shape=out_shapes,
        compiler_params=pltpu.CompilerParams(
            dimension_semantics=("arbitrary",), vmem_limit_bytes=VMEM_LIMIT_BYTES),
        name="head_proj",
    )(h, gain.reshape(1, d), w.astype(_BF16), hg, gather, spread)


def _out_proj_kernel(h_ref, o_ref, w_ref, out_ref):
    out_ref[...] = h_ref[...] + _dot(o_ref[...], w_ref[...])


def _out_proj(h, o, w):
    t, d = h.shape
    row_spec = pl.BlockSpec((PROJ_ROWS, d), lambda i: (i, 0))
    return pl.pallas_call(
        _out_proj_kernel,
        grid=(t // PROJ_ROWS,),
        in_specs=[row_spec, row_spec, _resident(w.shape)],
        out_specs=row_spec,
        out_shape=jax.ShapeDtypeStruct(h.shape, h.dtype),
        compiler_params=pltpu.CompilerParams(
            dimension_semantics=("arbitrary",), vmem_limit_bytes=VMEM_LIMIT_BYTES),
        name="out_proj",
    )(h, o, w.astype(_BF16))


def _attn_kernel(q_ref, k_ref, v_ref, o_ref, acc_ref, surv_ref):
    i = pl.program_id(2)
    q = q_ref[0]
    lane = lax.broadcasted_iota(jnp.int32, (Q_BLOCK, LANES), 1)
    first_head = lane < HEAD_DIM
    zero = jnp.zeros_like(q)
    q2 = jnp.concatenate([jnp.where(first_head, q, zero), jnp.where(first_head, zero, q)], axis=0)

    src = lax.broadcasted_iota(jnp.int32, (Q_BLOCK, 2 * Q_BLOCK), 0)
    dst = lax.broadcasted_iota(jnp.int32, (Q_BLOCK, 2 * Q_BLOCK), 1)
    suffix_mat = ((src > dst) | (dst >= Q_BLOCK)).astype(_BF16)

    t_idx = lax.broadcasted_iota(jnp.int32, (2 * Q_BLOCK, Q_BLOCK), 0) % Q_BLOCK
    s_idx = lax.broadcasted_iota(jnp.int32, (2 * Q_BLOCK, Q_BLOCK), 1)
    causal = s_idx < t_idx

    def visit(j, mask):
        start = pl.multiple_of(j * Q_BLOCK, Q_BLOCK)
        kj = k_ref[0, pl.ds(start, Q_BLOCK), :]
        vj = v_ref[0, pl.ds(start, Q_BLOCK), :]
        z = lax.dot_general(q2, kj, (((1,), (1,)), ((), ())), preferred_element_type=_F32)
        lneg = -(jnp.maximum(z, 0.0) + jnp.log1p(jnp.exp(-jnp.abs(z))))
        if mask is not None:
            lneg = jnp.where(mask, lneg, 0.0)
        hi, lo = _split_bf16(lneg)
        sums = _dot(hi, suffix_mat) + _dot(lo, suffix_mat)
        surv = surv_ref[...]
        a = jnp.exp(z + lneg + sums[:, :Q_BLOCK] + surv)
        if mask is not None:
            a = jnp.where(mask, a, 0.0)
        a = a.astype(_BF16)
        a_wide = jnp.concatenate([a[:Q_BLOCK], a[Q_BLOCK:]], axis=1)
        vzero = jnp.zeros_like(vj)
        v_tall = jnp.concatenate(
            [jnp.where(first_head, vj, vzero), jnp.where(first_head, vzero, vj)], axis=0)
        acc_ref[...] += _dot(a_wide, v_tall)
        surv = surv + sums[:, Q_BLOCK:]
        surv_ref[...] = surv
        return jnp.max(surv) > LOG_SURVIVAL_FLOOR

    acc_ref[...] = jnp.zeros_like(acc_ref)
    surv_ref[...] = jnp.zeros_like(surv_ref)
    alive = visit(i, causal)

    def cond(carry):
        j, alive = carry
        return jnp.logical_and(j >= 0, alive)

    def body(carry):
        j, _ = carry
        return j - 1, visit(j, None)

    lax.while_loop(cond, body, (i - 1, alive))
    o_ref[0] = acc_ref[...].astype(o_ref.dtype)


def _attention(q3, k3, v3):
    b, s, d = q3.shape
    assert s % Q_BLOCK == 0 and d % LANES == 0
    q_spec = pl.BlockSpec((1, Q_BLOCK, LANES), lambda bi, hp, i: (bi, i, hp))
    kv_spec = pl.BlockSpec((1, s, LANES), lambda bi, hp, i: (bi, 0, hp))
    return pl.pallas_call(
        _attn_kernel,
        grid=(b, d // LANES, s // Q_BLOCK),
        in_specs=[q_spec, kv_spec, kv_spec],
        out_specs=q_spec,
        out_shape=jax.ShapeDtypeStruct(q3.shape, _BF16),
        scratch_shapes=[pltpu.VMEM((Q_BLOCK, LANES), _F32),
                        pltpu.VMEM((HEADS_PER_STEP * Q_BLOCK, Q_BLOCK), _F32)],
        compiler_params=pltpu.CompilerParams(
            dimension_semantics=("arbitrary", "arbitrary", "arbitrary"),
            vmem_limit_bytes=VMEM_LIMIT_BYTES),
        name="attn",
    )(q3, k3, v3)


def kernel(x, ffn1_norm, ffn1_w_in, ffn1_w_out, ffn2_norm, ffn2_w_in, ffn2_w_out, pool_norm,
           pool_w, pool_scale, kv_norm, w_kv, k_gain, attn_norm, w_q, q_gain, w_o):
    b, s, d = x.shape
    assert d == N_HEADS * HEAD_DIM
    h = x.reshape(b * s, d)

    h = _ffn(h, ffn1_norm[0], ffn1_w_in[0], ffn1_w_out[0])
    h = _pool(h.reshape(b, s, d), pool_norm[0], pool_w[0], pool_scale[0]).reshape(b * s, d)
    h = _ffn(h, ffn2_norm[0], ffn2_w_in[0], ffn2_w_out[0])

    k, v = _head_proj(h, kv_norm, w_kv, k_gain, d_norm=d)

    h = _ffn(h, ffn1_norm[1], ffn1_w_in[1], ffn1_w_out[1])
    (q,) = _head_proj(h, attn_norm[0], w_q[0], q_gain[0] * (HEAD_DIM ** -0.5), d_norm=d)
    o = _attention(q.reshape(b, s, d), k.reshape(b, s, d), v.reshape(b, s, d))
    h = _out_proj(h, o.reshape(b * s, d), w_o[0])
    h = _ffn(h, ffn2_norm[1], ffn2_w_in[1], ffn2_w_out[1])
    return h.reshape(b, s, d)
```

```python
import functools

import jax
import jax.numpy as jnp
from jax import lax
from jax.experimental import pallas as pl
from jax.experimental.pallas import tpu as pltpu

N_HEADS = 16
HEAD_DIM = 64
POOL_WINDOWS = (2, 4, 8, 16)
MAX_WINDOW = max(POOL_WINDOWS)
Q_BLOCK = 128
RMS_EPS = 1e-6
FFN_RES_WEIGHT = 0.5

LANES = 128
MXU_DIM = 256
VMEM_LIMIT_BYTES = 56 * 1024 * 1024

FFN_ROWS = 512
FFN_CHUNK = MXU_DIM
PROJ_ROWS = 512
POOL_ROWS = 512
HEADS_PER_STEP = LANES // HEAD_DIM
ATTN_Q_BLOCKS = 4
ATTN_UNROLLED_VISITS = 3

DEAD_LOG = 105.0
LOG2_E = 1.4426950408889634

_BF16 = jnp.bfloat16
_F32 = jnp.float32


def _resident(shape):
    zeros = (0,) * len(shape)
    return pl.BlockSpec(shape, lambda *_: zeros, pipeline_mode=pl.Buffered(1))


def _rms_norm(x, gain):
    ms = jnp.mean(x * x, axis=-1, keepdims=True)
    return x * lax.rsqrt(ms + RMS_EPS) * gain


def _dot(a, b):
    return jnp.dot(a, b, preferred_element_type=_F32)


def _split_bf16(x):
    hi = x.astype(_BF16)
    lo = (x - hi.astype(_F32)).astype(_BF16)
    return hi, lo


def _ffn_update(x, g_ref, win_ref, wout_ref, o_ref):
    d_ff = wout_ref.shape[0]
    xn = _rms_norm(x, g_ref[...]).astype(_BF16)
    acc = jnp.zeros(x.shape, _F32)
    for c in range(d_ff // FFN_CHUNK):
        lo = c * FFN_CHUNK
        gate = _dot(xn, win_ref[:, lo:lo + FFN_CHUNK])
        up = _dot(xn, win_ref[:, d_ff + lo:d_ff + lo + FFN_CHUNK])
        act = (gate * jax.nn.sigmoid(gate) * up).astype(_BF16)
        acc = acc + _dot(act, wout_ref[lo:lo + FFN_CHUNK, :])
    o_ref[...] = x + FFN_RES_WEIGHT * acc


def _ffn_kernel(x_ref, g_ref, win_ref, wout_ref, o_ref):
    _ffn_update(x_ref[...], g_ref, win_ref, wout_ref, o_ref)


def _proj_ffn_kernel(x_ref, a_ref, wp_ref, g_ref, win_ref, wout_ref, o_ref):
    _ffn_update(x_ref[...] + _dot(a_ref[...], wp_ref[...]), g_ref, win_ref, wout_ref, o_ref)


def _ffn(h, gain, w_in, w_out, mixer_out=None, w_proj=None):
    t, d = h.shape
    d_ff = w_out.shape[0]
    assert t % FFN_ROWS == 0 and d_ff % FFN_CHUNK == 0 and w_in.shape == (d, 2 * d_ff)
    row_spec = pl.BlockSpec((FFN_ROWS, d), lambda i: (i, 0))
    ffn_specs = [_resident((1, d)), _resident(w_in.shape), _resident(w_out.shape)]
    ffn_args = (gain.reshape(1, d), w_in.astype(_BF16), w_out.astype(_BF16))
    if mixer_out is None:
        body, specs, args = _ffn_kernel, [row_spec], (h,)
    else:
        body = _proj_ffn_kernel
        specs = [row_spec, pl.BlockSpec((FFN_ROWS, w_proj.shape[0]), lambda i: (i, 0)),
                 _resident(w_proj.shape)]
        args = (h, mixer_out, w_proj.astype(_BF16))
    return pl.pallas_call(
        body,
        grid=(t // FFN_ROWS,),
        in_specs=specs + ffn_specs,
        out_specs=row_spec,
        out_shape=jax.ShapeDtypeStruct(h.shape, h.dtype),
        compiler_params=pltpu.CompilerParams(
            dimension_semantics=("arbitrary",), vmem_limit_bytes=VMEM_LIMIT_BYTES),
        name="ffn" if mixer_out is None else "proj_ffn",
    )(*args, *ffn_args)


def _pool_kernel(x_ref, halo_ref, g_ref, w_ref, scale_ref, o_ref):
    i = pl.program_id(1)
    x = x_ref[0]
    gain = g_ref[...]
    xn = _rms_norm(x, gain)
    halo = jnp.where(i > 0, _rms_norm(halo_ref[0], gain), 0.0)
    ext = jnp.concatenate([halo, xn], axis=0)
    rows = x.shape[0]
    pos = i * rows + lax.broadcasted_iota(jnp.int32, (rows, 1), 0)
    group_w = w_ref.shape[1]
    outs = []
    for g, w in enumerate(POOL_WINDOWS):
        cols = slice(g * group_w, (g + 1) * group_w)
        s = ext[:, cols]
        shift = 1
        while shift < w:
            s = s + pltpu.roll(s, shift, axis=0)
            shift *= 2
        inv_cnt = 1.0 / jnp.minimum(pos + 1, w).astype(_F32)
        p = s[MAX_WINDOW:] * inv_cnt - xn[:, cols]
        outs.append(_dot(p.astype(_BF16), w_ref[g]))
    y = jnp.concatenate(outs, axis=-1)
    o_ref[0] = x + y * scale_ref[...]


def _pool(h3, gain, w_groups, scale):
    b, s, d = h3.shape
    assert s % POOL_ROWS == 0 and POOL_ROWS % MAX_WINDOW == 0
    halo_blocks = POOL_ROWS // MAX_WINDOW
    row_spec = pl.BlockSpec((1, POOL_ROWS, d), lambda bi, i: (bi, i, 0))
    halo_spec = pl.BlockSpec(
        (1, MAX_WINDOW, d), lambda bi, i: (bi, jnp.maximum(i * halo_blocks - 1, 0), 0))
    return pl.pallas_call(
        _pool_kernel,
        grid=(b, s // POOL_ROWS),
        in_specs=[row_spec, halo_spec, _resident((1, d)), _resident(w_groups.shape),
                  _resident((1, d))],
        out_specs=row_spec,
        out_shape=jax.ShapeDtypeStruct(h3.shape, h3.dtype),
        compiler_params=pltpu.CompilerParams(
            dimension_semantics=("arbitrary", "arbitrary"), vmem_limit_bytes=VMEM_LIMIT_BYTES),
        name="pool",
    )(h3, h3, gain.reshape(1, d), w_groups.astype(_BF16), scale.reshape(1, d))


def _head_sum_matrices(d):
    head_of_col = jnp.arange(d) // HEAD_DIM
    gather = (head_of_col[:, None] == jnp.arange(LANES)[None, :]).astype(_BF16)
    return gather, jnp.concatenate([gather.T, gather.T], axis=0)


def _head_proj_kernel(x_ref, g_ref, w_ref, hg_ref, gather_ref, spread_ref, *o_refs, d_norm):
    xn = _rms_norm(x_ref[...], g_ref[...]).astype(_BF16)
    y = _dot(xn, w_ref[...])
    yn = y[:, :d_norm]
    head_ss = _dot((yn * yn).astype(_BF16), gather_ref[...])
    r = lax.rsqrt(head_ss * (1.0 / HEAD_DIM) + RMS_EPS)
    r_hi, r_lo = _split_bf16(r)
    r_full = _dot(jnp.concatenate([r_hi, r_lo], axis=1), spread_ref[...])
    o_refs[0][...] = (yn * r_full * hg_ref[...]).astype(o_refs[0].dtype)
    if len(o_refs) > 1:
        o_refs[1][...] = y[:, d_norm:].astype(o_refs[1].dtype)


def _head_proj(h, gain, w, head_gain, d_norm):
    t, d = h.shape
    d_out = w.shape[1]
    assert t % PROJ_ROWS == 0 and d_norm % HEAD_DIM == 0 and d_norm // HEAD_DIM <= LANES
    gather, spread = _head_sum_matrices(d_norm)
    hg = jnp.tile(head_gain, d_norm // HEAD_DIM).reshape(1, d_norm)
    row_spec = pl.BlockSpec((PROJ_ROWS, d), lambda i: (i, 0))
    out_shapes = [jax.ShapeDtypeStruct((t, d_norm), _BF16)]
    out_specs = [pl.BlockSpec((PROJ_ROWS, d_norm), lambda i: (i, 0))]
    if d_out > d_norm:
        out_shapes.append(jax.ShapeDtypeStruct((t, d_out - d_norm), _BF16))
        out_specs.append(pl.BlockSpec((PROJ_ROWS, d_out - d_norm), lambda i: (i, 0)))
    return pl.pallas_call(
        functools.partial(_head_proj_kernel, d_norm=d_norm),
        grid=(t // PROJ_ROWS,),
        in_specs=[row_spec, _resident((1, d)), _resident(w.shape), _resident((1, d_norm)),
                  _resident(gather.shape), _resident(spread.shape)],
        out_specs=out_specs,
        out_shape=out_shapes,
        compiler_params=pltpu.CompilerParams(
            dimension_semantics=("arbitrary",), vmem_limit_bytes=VMEM_LIMIT_BYTES),
        name="head_proj",
    )(h, gain.reshape(1, d), w.astype(_BF16), hg, gather, spread)


def _attn_kernel(q_ref, k_ref, v_ref, o_ref, acc_ref, dead_ref):
    ig = pl.program_id(2)
    lane = lax.broadcasted_iota(jnp.int32, (Q_BLOCK, LANES), 1)
    first_head = lane < HEAD_DIM

    src = lax.broadcasted_iota(jnp.int32, (2 * Q_BLOCK, Q_BLOCK), 0) % Q_BLOCK
    dst = lax.broadcasted_iota(jnp.int32, (2 * Q_BLOCK, Q_BLOCK), 1)
    suffix_mat = (src > dst).astype(_BF16)

    t_idx = lax.broadcasted_iota(jnp.int32, (2 * Q_BLOCK, Q_BLOCK), 0) % Q_BLOCK
    s_idx = lax.broadcasted_iota(jnp.int32, (2 * Q_BLOCK, Q_BLOCK), 1)
    causal = s_idx < t_idx

    def stack_heads(x):
        zero = jnp.zeros_like(x)
        return jnp.concatenate(
            [jnp.where(first_head, x, zero), jnp.where(first_head, zero, x)], axis=0)

    def rows_of(parts):
        return parts[0] if len(parts) == 1 else jnp.concatenate(parts, axis=0)

    def load_kv(j):
        start = j * Q_BLOCK
        if not isinstance(start, int):
            start = pl.multiple_of(start, Q_BLOCK)
        rows = pl.ds(start, Q_BLOCK)
        return k_ref[0, rows, :], stack_heads(v_ref[0, rows, :])

    def score(q2, kj):
        return lax.dot_general(q2, kj, (((1,), (1,)), ((), ())), preferred_element_type=_F32)

    def softplus(z, mask):
        y = jnp.maximum(z, 0.0) + jnp.log(1.0 + jnp.exp2(jnp.abs(z) * (-LOG2_E)))
        return y if mask is None else jnp.where(mask, y, 0.0)

    def later_sums(ys):
        out = _dot(rows_of([jnp.concatenate(_split_bf16(y), axis=1) for y in ys]), suffix_mat)
        return [out[n * 2 * Q_BLOCK:(n + 1) * 2 * Q_BLOCK] for n in range(len(ys))]

    def weights(z, y, later, dead, mask):
        drop = y + later
        if dead is not None:
            drop = drop + dead
        a = jnp.exp(z - drop)
        if mask is not None:
            a = jnp.where(mask, a, 0.0)
        a = a.astype(_BF16)
        return jnp.concatenate([a[:Q_BLOCK], a[Q_BLOCK:]], axis=1)

    def unrolled_visits(first_block):
        users = {}
        for g in range(ATTN_Q_BLOCKS):
            for back in range(ATTN_UNROLLED_VISITS):
                if isinstance(first_block, int) and first_block + g - back < 0:
                    continue
                users.setdefault(g - back, []).append((g, back))
        offsets = sorted(users, reverse=True)
        order = [v for e in offsets for v in users[e]]
        q2 = [stack_heads(q_ref[0, g * Q_BLOCK:(g + 1) * Q_BLOCK, :])
              for g in range(ATTN_Q_BLOCKS)]
        kv = {e: load_kv(first_block + e) for e in offsets}

        z = {}
        for e in offsets:
            zc = score(rows_of([q2[g] for g, _ in users[e]]), kv[e][0])
            for n, v in enumerate(users[e]):
                z[v] = zc[n * 2 * Q_BLOCK:(n + 1) * 2 * Q_BLOCK]
        y = {v: softplus(z[v], causal if v[1] == 0 else None) for v in order}
        later = dict(zip(order, later_sums([y[v] for v in order])))
        total = {v: jnp.sum(y[v], axis=-1, keepdims=True) for v in order}

        a_wide, dead_end = {}, {}
        for g in range(ATTN_Q_BLOCKS):
            dead = None
            for back in range(ATTN_UNROLLED_VISITS):
                v = (g, back)
                if v in z:
                    a_wide[v] = weights(z[v], y[v], later[v], dead, causal if back == 0 else None)
                    dead = total[v] if dead is None else dead + total[v]
            dead_end[g] = dead

        acc = {}
        for e in offsets:
            c = _dot(rows_of([a_wide[v] for v in users[e]]), kv[e][1])
            for n, (g, _) in enumerate(users[e]):
                part = c[n * Q_BLOCK:(n + 1) * Q_BLOCK]
                acc[g] = part if g not in acc else acc[g] + part

        most_alive = None
        for g in range(ATTN_Q_BLOCKS):
            acc_ref[g] = acc[g]
            dead_ref[g] = dead_end[g]
            g_min = jnp.min(dead_end[g])
            most_alive = g_min if most_alive is None else jnp.minimum(most_alive, g_min)
        return most_alive < DEAD_LOG

    def visit(q2, kv, dead):
        z = score(q2, kv[0])
        y = softplus(z, None)
        a = weights(z, y, later_sums([y])[0], dead, None)
        return _dot(a, kv[1]), dead + jnp.sum(y, axis=-1, keepdims=True)

    def remaining_visits(first_block):
        for g in range(ATTN_Q_BLOCKS):
            next_block = first_block + g - ATTN_UNROLLED_VISITS
            if isinstance(next_block, int) and next_block < 0:
                continue
            q2 = stack_heads(q_ref[0, g * Q_BLOCK:(g + 1) * Q_BLOCK, :])

            def cond(carry):
                j, alive = carry
                return jnp.logical_and(j >= 0, alive)

            def body(carry, g=g, q2=q2):
                j, _ = carry
                contrib, dead = visit(q2, load_kv(j), dead_ref[g])
                acc_ref[g] += contrib
                dead_ref[g] = dead
                return j - 1, jnp.min(dead) < DEAD_LOG

            lax.while_loop(cond, body, (next_block, jnp.min(dead_ref[g]) < DEAD_LOG))

    @pl.when(ig == 0)
    def _():
        pl.when(unrolled_visits(0))(lambda: remaining_visits(0))

    @pl.when(ig > 0)
    def _():
        first_block = ig * ATTN_Q_BLOCKS
        pl.when(unrolled_visits(first_block))(lambda: remaining_visits(first_block))

    for g in range(ATTN_Q_BLOCKS):
        o_ref[0, g * Q_BLOCK:(g + 1) * Q_BLOCK, :] = acc_ref[g].astype(o_ref.dtype)


def _attention(q3, k3, v3):
    b, s, d = q3.shape
    group_rows = ATTN_Q_BLOCKS * Q_BLOCK
    assert ATTN_Q_BLOCKS >= ATTN_UNROLLED_VISITS - 1
    assert s % group_rows == 0 and d % LANES == 0
    q_spec = pl.BlockSpec((1, group_rows, LANES), lambda bi, hp, i: (bi, i, hp))
    kv_spec = pl.BlockSpec((1, s, LANES), lambda bi, hp, i: (bi, 0, hp))
    return pl.pallas_call(
        _attn_kernel,
        grid=(b, d // LANES, s // group_rows),
        in_specs=[q_spec, kv_spec, kv_spec],
        out_specs=q_spec,
        out_shape=jax.ShapeDtypeStruct(q3.shape, _BF16),
        scratch_shapes=[pltpu.VMEM((ATTN_Q_BLOCKS, Q_BLOCK, LANES), _F32),
                        pltpu.VMEM((ATTN_Q_BLOCKS, HEADS_PER_STEP * Q_BLOCK, 1), _F32)],
        compiler_params=pltpu.CompilerParams(
            dimension_semantics=("arbitrary", "arbitrary", "arbitrary"),
            vmem_limit_bytes=VMEM_LIMIT_BYTES),
        name="attn",
    )(q3, k3, v3)


def kernel(x, ffn1_norm, ffn1_w_in, ffn1_w_out, ffn2_norm, ffn2_w_in, ffn2_w_out, pool_norm,
           pool_w, pool_scale, kv_norm, w_kv, k_gain, attn_norm, w_q, q_gain, w_o):
    b, s, d = x.shape
    assert d == N_HEADS * HEAD_DIM
    h = x.reshape(b * s, d)

    h = _ffn(h, ffn1_norm[0], ffn1_w_in[0], ffn1_w_out[0])
    h = _pool(h.reshape(b, s, d), pool_norm[0], pool_w[0], pool_scale[0]).reshape(b * s, d)
    h = _ffn(h, ffn2_norm[0], ffn2_w_in[0], ffn2_w_out[0])

    k, v = _head_proj(h, kv_norm, w_kv, k_gain, d_norm=d)

    h = _ffn(h, ffn1_norm[1], ffn1_w_in[1], ffn1_w_out[1])
    (q,) = _head_proj(h, attn_norm[0], w_q[0], q_gain[0] * (HEAD_DIM ** -0.5), d_norm=d)
    o = _attention(q.reshape(b, s, d), k.reshape(b, s, d), v.reshape(b, s, d))
    h = _ffn(h, ffn2_norm[1], ffn2_w_in[1], ffn2_w_out[1],
             mixer_out=o.reshape(b * s, d), w_proj=w_o[0])
    return h.reshape(b, s, d)
```

```python
import functools

import jax
import jax.numpy as jnp
from jax import lax
from jax.experimental import pallas as pl
from jax.experimental.pallas import tpu as pltpu

N_HEADS = 16
HEAD_DIM = 64
POOL_WINDOWS = (2, 4, 8, 16)
MAX_WINDOW = max(POOL_WINDOWS)
Q_BLOCK = 128
RMS_EPS = 1e-6
FFN_RES_WEIGHT = 0.5

LANES = 128
MXU_DIM = 256
VMEM_LIMIT_BYTES = 56 * 1024 * 1024

FFN_ROWS = 1024
FFN_CHUNK = MXU_DIM
PROJ_ROWS = 1024
POOL_ROWS = 512
HEADS_PER_STEP = LANES // HEAD_DIM
ATTN_Q_BLOCKS = 8
ATTN_UNROLLED_VISITS = 3

DEAD_LOG = 105.0
LOG2_E = 1.4426950408889634

_BF16 = jnp.bfloat16
_F32 = jnp.float32


def _resident(shape):
    zeros = (0,) * len(shape)
    return pl.BlockSpec(shape, lambda *_: zeros, pipeline_mode=pl.Buffered(1))


def _rms_norm(x, gain):
    ms = jnp.mean(x * x, axis=-1, keepdims=True)
    return x * lax.rsqrt(ms + RMS_EPS) * gain


def _dot(a, b):
    return jnp.dot(a, b, preferred_element_type=_F32)


def _split_bf16(x):
    hi = x.astype(_BF16)
    lo = (x - hi.astype(_F32)).astype(_BF16)
    return hi, lo


def _ffn_update(x, g_ref, win_ref, wout_ref, o_ref):
    d_ff = wout_ref.shape[0]
    xn = _rms_norm(x, g_ref[...]).astype(_BF16)
    acc = jnp.zeros(x.shape, _F32)
    for c in range(d_ff // FFN_CHUNK):
        lo = c * FFN_CHUNK
        gate = _dot(xn, win_ref[:, lo:lo + FFN_CHUNK])
        up = _dot(xn, win_ref[:, d_ff + lo:d_ff + lo + FFN_CHUNK])
        act = (gate * jax.nn.sigmoid(gate) * up).astype(_BF16)
        acc = acc + _dot(act, wout_ref[lo:lo + FFN_CHUNK, :])
    o_ref[...] = x + FFN_RES_WEIGHT * acc


def _ffn_kernel(x_ref, g_ref, win_ref, wout_ref, o_ref):
    _ffn_update(x_ref[...], g_ref, win_ref, wout_ref, o_ref)


def _proj_ffn_kernel(x_ref, a_ref, wp_ref, g_ref, win_ref, wout_ref, o_ref):
    _ffn_update(x_ref[...] + _dot(a_ref[...], wp_ref[...]), g_ref, win_ref, wout_ref, o_ref)


def _ffn(h, gain, w_in, w_out, mixer_out=None, w_proj=None):
    t, d = h.shape
    d_ff = w_out.shape[0]
    assert t % FFN_ROWS == 0 and d_ff % FFN_CHUNK == 0 and w_in.shape == (d, 2 * d_ff)
    row_spec = pl.BlockSpec((FFN_ROWS, d), lambda i: (i, 0))
    ffn_specs = [_resident((1, d)), _resident(w_in.shape), _resident(w_out.shape)]
    ffn_args = (gain.reshape(1, d), w_in.astype(_BF16), w_out.astype(_BF16))
    if mixer_out is None:
        body, specs, args = _ffn_kernel, [row_spec], (h,)
    else:
        body = _proj_ffn_kernel
        specs = [row_spec, pl.BlockSpec((FFN_ROWS, w_proj.shape[0]), lambda i: (i, 0)),
                 _resident(w_proj.shape)]
        args = (h, mixer_out, w_proj.astype(_BF16))
    return pl.pallas_call(
        body,
        grid=(t // FFN_ROWS,),
        in_specs=specs + ffn_specs,
        out_specs=row_spec,
        out_shape=jax.ShapeDtypeStruct(h.shape, h.dtype),
        compiler_params=pltpu.CompilerParams(
            dimension_semantics=("arbitrary",), vmem_limit_bytes=VMEM_LIMIT_BYTES),
        name="ffn" if mixer_out is None else "proj_ffn",
    )(*args, *ffn_args)


def _pool_kernel(x_ref, halo_ref, g_ref, w_ref, scale_ref, o_ref):
    i = pl.program_id(1)
    x = x_ref[0]
    gain = g_ref[...]
    xn = _rms_norm(x, gain)
    halo = jnp.where(i > 0, _rms_norm(halo_ref[0], gain), 0.0)
    ext = jnp.concatenate([halo, xn], axis=0)
    rows = x.shape[0]
    pos = i * rows + lax.broadcasted_iota(jnp.int32, (rows, 1), 0)
    group_w = w_ref.shape[1]
    outs = []
    for g, w in enumerate(POOL_WINDOWS):
        cols = slice(g * group_w, (g + 1) * group_w)
        s = ext[:, cols]
        shift = 1
        while shift < w:
            s = s + pltpu.roll(s, shift, axis=0)
            shift *= 2
        inv_cnt = 1.0 / jnp.minimum(pos + 1, w).astype(_F32)
        p = s[MAX_WINDOW:] * inv_cnt - xn[:, cols]
        outs.append(_dot(p.astype(_BF16), w_ref[g]))
    y = jnp.concatenate(outs, axis=-1)
    o_ref[0] = x + y * scale_ref[...]


def _pool(h3, gain, w_groups, scale):
    b, s, d = h3.shape
    assert s % POOL_ROWS == 0 and POOL_ROWS % MAX_WINDOW == 0
    halo_blocks = POOL_ROWS // MAX_WINDOW
    row_spec = pl.BlockSpec((1, POOL_ROWS, d), lambda bi, i: (bi, i, 0))
    halo_spec = pl.BlockSpec(
        (1, MAX_WINDOW, d), lambda bi, i: (bi, jnp.maximum(i * halo_blocks - 1, 0), 0))
    return pl.pallas_call(
        _pool_kernel,
        grid=(b, s // POOL_ROWS),
        in_specs=[row_spec, halo_spec, _resident((1, d)), _resident(w_groups.shape),
                  _resident((1, d))],
        out_specs=row_spec,
        out_shape=jax.ShapeDtypeStruct(h3.shape, h3.dtype),
        compiler_params=pltpu.CompilerParams(
            dimension_semantics=("arbitrary", "arbitrary"), vmem_limit_bytes=VMEM_LIMIT_BYTES),
        name="pool",
    )(h3, h3, gain.reshape(1, d), w_groups.astype(_BF16), scale.reshape(1, d))


def _head_sum_matrices(d):
    head_of_col = jnp.arange(d) // HEAD_DIM
    gather = (head_of_col[:, None] == jnp.arange(LANES)[None, :]).astype(_BF16)
    return gather, jnp.concatenate([gather.T, gather.T], axis=0)


def _head_proj_kernel(x_ref, g_ref, w_ref, hg_ref, gather_ref, spread_ref, *o_refs, d_norm):
    xn = _rms_norm(x_ref[...], g_ref[...]).astype(_BF16)
    y = _dot(xn, w_ref[...])
    yn = y[:, :d_norm]
    head_ss = _dot((yn * yn).astype(_BF16), gather_ref[...])
    r = lax.rsqrt(head_ss * (1.0 / HEAD_DIM) + RMS_EPS)
    r_hi, r_lo = _split_bf16(r)
    r_full = _dot(jnp.concatenate([r_hi, r_lo], axis=1), spread_ref[...])
    o_refs[0][...] = (yn * r_full * hg_ref[...]).astype(o_refs[0].dtype)
    if len(o_refs) > 1:
        o_refs[1][...] = y[:, d_norm:].astype(o_refs[1].dtype)


def _head_proj(h, gain, w, head_gain, d_norm):
    t, d = h.shape
    d_out = w.shape[1]
    assert t % PROJ_ROWS == 0 and d_norm % HEAD_DIM == 0 and d_norm // HEAD_DIM <= LANES
    gather, spread = _head_sum_matrices(d_norm)
    hg = jnp.tile(head_gain, d_norm // HEAD_DIM).reshape(1, d_norm)
    row_spec = pl.BlockSpec((PROJ_ROWS, d), lambda i: (i, 0))
    out_shapes = [jax.ShapeDtypeStruct((t, d_norm), _BF16)]
    out_specs = [pl.BlockSpec((PROJ_ROWS, d_norm), lambda i: (i, 0))]
    if d_out > d_norm:
        out_shapes.append(jax.ShapeDtypeStruct((t, d_out - d_norm), _BF16))
        out_specs.append(pl.BlockSpec((PROJ_ROWS, d_out - d_norm), lambda i: (i, 0)))
    return pl.pallas_call(
        functools.partial(_head_proj_kernel, d_norm=d_norm),
        grid=(t // PROJ_ROWS,),
        in_specs=[row_spec, _resident((1, d)), _resident(w.shape), _resident((1, d_norm)),
                  _resident(gather.shape), _resident(spread.shape)],
        out_specs=out_specs,
        out_shape=out_shapes,
        compiler_params=pltpu.CompilerParams(
            dimension_semantics=("arbitrary",), vmem_limit_bytes=VMEM_LIMIT_BYTES),
        name="head_proj",
    )(h, gain.reshape(1, d), w.astype(_BF16), hg, gather, spread)


def _attn_kernel(q_ref, k_ref, v_ref, o_ref, acc_ref, dead_ref):
    ig = pl.program_id(2)
    lane = lax.broadcasted_iota(jnp.int32, (Q_BLOCK, LANES), 1)
    first_head = lane < HEAD_DIM

    src = lax.broadcasted_iota(jnp.int32, (Q_BLOCK, Q_BLOCK), 0)
    dst = lax.broadcasted_iota(jnp.int32, (Q_BLOCK, Q_BLOCK), 1)
    suffix_mat = (src > dst).astype(_BF16)

    t_idx = lax.broadcasted_iota(jnp.int32, (2 * Q_BLOCK, Q_BLOCK), 0) % Q_BLOCK
    s_idx = lax.broadcasted_iota(jnp.int32, (2 * Q_BLOCK, Q_BLOCK), 1)
    causal = s_idx < t_idx

    def stack_heads(x):
        zero = jnp.zeros_like(x)
        return jnp.concatenate(
            [jnp.where(first_head, x, zero), jnp.where(first_head, zero, x)], axis=0)

    def rows_of(parts):
        return parts[0] if len(parts) == 1 else jnp.concatenate(parts, axis=0)

    def load_kv(j):
        start = j * Q_BLOCK
        if not isinstance(start, int):
            start = pl.multiple_of(start, Q_BLOCK)
        rows = pl.ds(start, Q_BLOCK)
        return k_ref[0, rows, :], stack_heads(v_ref[0, rows, :])

    def score(q2, kj):
        return lax.dot_general(q2, kj, (((1,), (1,)), ((), ())), preferred_element_type=_F32)

    def softplus(z, mask):
        y = jnp.maximum(z, 0.0) + jnp.log(1.0 + jnp.exp2(jnp.abs(z) * (-LOG2_E)))
        return y if mask is None else jnp.where(mask, y, 0.0)

    def later_sums(ys):
        out = _dot(rows_of([y.astype(_BF16) for y in ys]), suffix_mat)
        return [out[n * 2 * Q_BLOCK:(n + 1) * 2 * Q_BLOCK] for n in range(len(ys))]

    def weights(z, y, later, dead, mask):
        drop = y + later
        if dead is not None:
            drop = drop + dead
        a = jnp.exp(z - drop)
        if mask is not None:
            a = jnp.where(mask, a, 0.0)
        a = a.astype(_BF16)
        return jnp.concatenate([a[:Q_BLOCK], a[Q_BLOCK:]], axis=1)

    def unrolled_visits(first_block):
        users = {}
        for g in range(ATTN_Q_BLOCKS):
            for back in range(ATTN_UNROLLED_VISITS):
                if isinstance(first_block, int) and first_block + g - back < 0:
                    continue
                users.setdefault(g - back, []).append((g, back))
        offsets = sorted(users, reverse=True)
        order = [v for e in offsets for v in users[e]]
        q2 = [stack_heads(q_ref[0, g * Q_BLOCK:(g + 1) * Q_BLOCK, :])
              for g in range(ATTN_Q_BLOCKS)]
        kv = {e: load_kv(first_block + e) for e in offsets}

        z = {}
        for e in offsets:
            zc = score(rows_of([q2[g] for g, _ in users[e]]), kv[e][0])
            for n, v in enumerate(users[e]):
                z[v] = zc[n * 2 * Q_BLOCK:(n + 1) * 2 * Q_BLOCK]
        y = {v: softplus(z[v], causal if v[1] == 0 else None) for v in order}
        later = dict(zip(order, later_sums([y[v] for v in order])))
        total = {v: jnp.sum(y[v], axis=-1, keepdims=True) for v in order}

        a_wide, dead_end = {}, {}
        for g in range(ATTN_Q_BLOCKS):
            dead = None
            for back in range(ATTN_UNROLLED_VISITS):
                v = (g, back)
                if v in z:
                    a_wide[v] = weights(z[v], y[v], later[v], dead, causal if back == 0 else None)
                    dead = total[v] if dead is None else dead + total[v]
            dead_end[g] = dead

        acc = {}
        for e in offsets:
            c = _dot(rows_of([a_wide[v] for v in users[e]]), kv[e][1])
            for n, (g, _) in enumerate(users[e]):
                part = c[n * Q_BLOCK:(n + 1) * Q_BLOCK]
                acc[g] = part if g not in acc else acc[g] + part

        most_alive = None
        for g in range(ATTN_Q_BLOCKS):
            acc_ref[g] = acc[g]
            dead_ref[g] = dead_end[g]
            g_min = jnp.min(dead_end[g])
            most_alive = g_min if most_alive is None else jnp.minimum(most_alive, g_min)
        return most_alive < DEAD_LOG

    def visit(q2, kv, dead):
        z = score(q2, kv[0])
        y = softplus(z, None)
        a = weights(z, y, later_sums([y])[0], dead, None)
        return _dot(a, kv[1]), dead + jnp.sum(y, axis=-1, keepdims=True)

    def remaining_visits(first_block):
        for g in range(ATTN_Q_BLOCKS):
            next_block = first_block + g - ATTN_UNROLLED_VISITS
            if isinstance(next_block, int) and next_block < 0:
                continue
            q2 = stack_heads(q_ref[0, g * Q_BLOCK:(g + 1) * Q_BLOCK, :])

            def cond(carry):
                j, alive = carry
                return jnp.logical_and(j >= 0, alive)

            def body(carry, g=g, q2=q2):
                j, _ = carry
                contrib, dead = visit(q2, load_kv(j), dead_ref[g])
                acc_ref[g] += contrib
                dead_ref[g] = dead
                return j - 1, jnp.min(dead) < DEAD_LOG

            lax.while_loop(cond, body, (next_block, jnp.min(dead_ref[g]) < DEAD_LOG))

    @pl.when(ig == 0)
    def _():
        pl.when(unrolled_visits(0))(lambda: remaining_visits(0))

    @pl.when(ig > 0)
    def _():
        first_block = ig * ATTN_Q_BLOCKS
        pl.when(unrolled_visits(first_block))(lambda: remaining_visits(first_block))

    for g in range(ATTN_Q_BLOCKS):
        o_ref[0, g * Q_BLOCK:(g + 1) * Q_BLOCK, :] = acc_ref[g].astype(o_ref.dtype)


def _attention(q3, k3, v3):
    b, s, d = q3.shape
    group_rows = ATTN_Q_BLOCKS * Q_BLOCK
    assert ATTN_Q_BLOCKS >= ATTN_UNROLLED_VISITS - 1
    assert s % group_rows == 0 and d % LANES == 0
    q_spec = pl.BlockSpec((1, group_rows, LANES), lambda bi, hp, i: (bi, i, hp))
    kv_spec = pl.BlockSpec((1, s, LANES), lambda bi, hp, i: (bi, 0, hp))
    return pl.pallas_call(
        _attn_kernel,
        grid=(b, d // LANES, s // group_rows),
        in_specs=[q_spec, kv_spec, kv_spec],
        out_specs=q_spec,
        out_shape=jax.ShapeDtypeStruct(q3.shape, _BF16),
        scratch_shapes=[pltpu.VMEM((ATTN_Q_BLOCKS, Q_BLOCK, LANES), _F32),
                        pltpu.VMEM((ATTN_Q_BLOCKS, HEADS_PER_STEP * Q_BLOCK, 1), _F32)],
        compiler_params=pltpu.CompilerParams(
            dimension_semantics=("arbitrary", "arbitrary", "arbitrary"),
            vmem_limit_bytes=VMEM_LIMIT_BYTES),
        name="attn",
    )(q3, k3, v3)


def kernel(x, ffn1_norm, ffn1_w_in, ffn1_w_out, ffn2_norm, ffn2_w_in, ffn2_w_out, pool_norm,
           pool_w, pool_scale, kv_norm, w_kv, k_gain, attn_norm, w_q, q_gain, w_o):
    b, s, d = x.shape
    assert d == N_HEADS * HEAD_DIM
    h = x.reshape(b * s, d)

    h = _ffn(h, ffn1_norm[0], ffn1_w_in[0], ffn1_w_out[0])
    h = _pool(h.reshape(b, s, d), pool_norm[0], pool_w[0], pool_scale[0]).reshape(b * s, d)
    h = _ffn(h, ffn2_norm[0], ffn2_w_in[0], ffn2_w_out[0])

    k, v = _head_proj(h, kv_norm, w_kv, k_gain, d_norm=d)

    h = _ffn(h, ffn1_norm[1], ffn1_w_in[1], ffn1_w_out[1])
    (q,) = _head_proj(h, attn_norm[0], w_q[0], q_gain[0] * (HEAD_DIM ** -0.5), d_norm=d)
    o = _attention(q.reshape(b, s, d), k.reshape(b, s, d), v.reshape(b, s, d))
    h = _ffn(h, ffn2_norm[1], ffn2_w_in[1], ffn2_w_out[1],
             mixer_out=o.reshape(b * s, d), w_proj=w_o[0])
    return h.reshape(b, s, d)
```

```python
import functools

import jax
import jax.numpy as jnp
from jax import lax
from jax.experimental import pallas as pl
from jax.experimental.pallas import tpu as pltpu

N_HEADS = 16
HEAD_DIM = 64
POOL_WINDOWS = (2, 4, 8, 16)
MAX_WINDOW = max(POOL_WINDOWS)
Q_BLOCK = 128
RMS_EPS = 1e-6
FFN_RES_WEIGHT = 0.5

LANES = 128
BF16_SUBLANES = 16
MXU_DIM = 256
VMEM_LIMIT_BYTES = 56 * 1024 * 1024

FFN_ROWS = 1024
FFN_CHUNK = MXU_DIM
PROJ_ROWS = 1024
POOL_ROWS = 512
HEADS_PER_STEP = LANES // HEAD_DIM
ATTN_Q_BLOCKS = 8
ATTN_UNROLLED_VISITS = 3
ATTN_LAST_VISIT_ROWS = 32

DEAD_LOG = 105.0
LOG2_E = 1.4426950408889634

_BF16 = jnp.bfloat16
_F32 = jnp.float32


def _resident(shape):
    zeros = (0,) * len(shape)
    return pl.BlockSpec(shape, lambda *_: zeros, pipeline_mode=pl.Buffered(1))


def _rms_norm(x, gain):
    ms = jnp.mean(x * x, axis=-1, keepdims=True)
    return x * lax.rsqrt(ms + RMS_EPS) * gain


def _dot(a, b):
    return jnp.dot(a, b, preferred_element_type=_F32)


def _split_bf16(x):
    hi = x.astype(_BF16)
    lo = (x - hi.astype(_F32)).astype(_BF16)
    return hi, lo


def _ffn_update(x, g_ref, win_ref, wout_ref, o_ref):
    d_ff = wout_ref.shape[0]
    xn = _rms_norm(x, g_ref[...]).astype(_BF16)
    acc = jnp.zeros(x.shape, _F32)
    for c in range(d_ff // FFN_CHUNK):
        lo = c * FFN_CHUNK
        gate = _dot(xn, win_ref[:, lo:lo + FFN_CHUNK])
        up = _dot(xn, win_ref[:, d_ff + lo:d_ff + lo + FFN_CHUNK])
        act = (gate * jax.nn.sigmoid(gate) * up).astype(_BF16)
        acc = acc + _dot(act, wout_ref[lo:lo + FFN_CHUNK, :])
    o_ref[...] = x + FFN_RES_WEIGHT * acc


def _ffn_kernel(x_ref, g_ref, win_ref, wout_ref, o_ref):
    _ffn_update(x_ref[...], g_ref, win_ref, wout_ref, o_ref)


def _proj_ffn_kernel(x_ref, a_ref, wp_ref, g_ref, win_ref, wout_ref, o_ref):
    _ffn_update(x_ref[...] + _dot(a_ref[...], wp_ref[...]), g_ref, win_ref, wout_ref, o_ref)


def _ffn(h, gain, w_in, w_out, mixer_out=None, w_proj=None):
    t, d = h.shape
    d_ff = w_out.shape[0]
    assert t % FFN_ROWS == 0 and d_ff % FFN_CHUNK == 0 and w_in.shape == (d, 2 * d_ff)
    row_spec = pl.BlockSpec((FFN_ROWS, d), lambda i: (i, 0))
    ffn_specs = [_resident((1, d)), _resident(w_in.shape), _resident(w_out.shape)]
    ffn_args = (gain.reshape(1, d), w_in.astype(_BF16), w_out.astype(_BF16))
    if mixer_out is None:
        body, specs, args = _ffn_kernel, [row_spec], (h,)
    else:
        body = _proj_ffn_kernel
        specs = [row_spec, pl.BlockSpec((FFN_ROWS, w_proj.shape[0]), lambda i: (i, 0)),
                 _resident(w_proj.shape)]
        args = (h, mixer_out, w_proj.astype(_BF16))
    return pl.pallas_call(
        body,
        grid=(t // FFN_ROWS,),
        in_specs=specs + ffn_specs,
        out_specs=row_spec,
        out_shape=jax.ShapeDtypeStruct(h.shape, h.dtype),
        compiler_params=pltpu.CompilerParams(
            dimension_semantics=("arbitrary",), vmem_limit_bytes=VMEM_LIMIT_BYTES),
        name="ffn" if mixer_out is None else "proj_ffn",
    )(*args, *ffn_args)


def _pool_kernel(x_ref, halo_ref, g_ref, w_ref, scale_ref, o_ref):
    i = pl.program_id(1)
    x = x_ref[0]
    gain = g_ref[...]
    xn = _rms_norm(x, gain)
    halo = jnp.where(i > 0, _rms_norm(halo_ref[0], gain), 0.0)
    ext = jnp.concatenate([halo, xn], axis=0)
    rows = x.shape[0]
    pos = i * rows + lax.broadcasted_iota(jnp.int32, (rows, 1), 0)
    group_w = w_ref.shape[1]
    outs = []
    for g, w in enumerate(POOL_WINDOWS):
        cols = slice(g * group_w, (g + 1) * group_w)
        s = ext[:, cols]
        shift = 1
        while shift < w:
            s = s + pltpu.roll(s, shift, axis=0)
            shift *= 2
        inv_cnt = 1.0 / jnp.minimum(pos + 1, w).astype(_F32)
        p = s[MAX_WINDOW:] * inv_cnt - xn[:, cols]
        outs.append(_dot(p.astype(_BF16), w_ref[g]))
    y = jnp.concatenate(outs, axis=-1)
    o_ref[0] = x + y * scale_ref[...]


def _pool(h3, gain, w_groups, scale):
    b, s, d = h3.shape
    assert s % POOL_ROWS == 0 and POOL_ROWS % MAX_WINDOW == 0
    halo_blocks = POOL_ROWS // MAX_WINDOW
    row_spec = pl.BlockSpec((1, POOL_ROWS, d), lambda bi, i: (bi, i, 0))
    halo_spec = pl.BlockSpec(
        (1, MAX_WINDOW, d), lambda bi, i: (bi, jnp.maximum(i * halo_blocks - 1, 0), 0))
    return pl.pallas_call(
        _pool_kernel,
        grid=(b, s // POOL_ROWS),
        in_specs=[row_spec, halo_spec, _resident((1, d)), _resident(w_groups.shape),
                  _resident((1, d))],
        out_specs=row_spec,
        out_shape=jax.ShapeDtypeStruct(h3.shape, h3.dtype),
        compiler_params=pltpu.CompilerParams(
            dimension_semantics=("arbitrary", "arbitrary"), vmem_limit_bytes=VMEM_LIMIT_BYTES),
        name="pool",
    )(h3, h3, gain.reshape(1, d), w_groups.astype(_BF16), scale.reshape(1, d))


def _head_sum_matrices(d):
    head_of_col = jnp.arange(d) // HEAD_DIM
    gather = (head_of_col[:, None] == jnp.arange(LANES)[None, :]).astype(_BF16)
    return gather, jnp.concatenate([gather.T, gather.T], axis=0)


def _head_proj_kernel(x_ref, g_ref, w_ref, hg_ref, gather_ref, spread_ref, *o_refs, d_norm):
    xn = _rms_norm(x_ref[...], g_ref[...]).astype(_BF16)
    y = _dot(xn, w_ref[...])
    yn = y[:, :d_norm]
    head_ss = _dot((yn * yn).astype(_BF16), gather_ref[...])
    r = lax.rsqrt(head_ss * (1.0 / HEAD_DIM) + RMS_EPS)
    r_hi, r_lo = _split_bf16(r)
    r_full = _dot(jnp.concatenate([r_hi, r_lo], axis=1), spread_ref[...])
    o_refs[0][...] = (yn * r_full * hg_ref[...]).astype(o_refs[0].dtype)
    if len(o_refs) > 1:
        o_refs[1][...] = y[:, d_norm:].astype(o_refs[1].dtype)


def _head_proj(h, gain, w, head_gain, d_norm):
    t, d = h.shape
    d_out = w.shape[1]
    assert t % PROJ_ROWS == 0 and d_norm % HEAD_DIM == 0 and d_norm // HEAD_DIM <= LANES
    gather, spread = _head_sum_matrices(d_norm)
    hg = jnp.tile(head_gain, d_norm // HEAD_DIM).reshape(1, d_norm)
    row_spec = pl.BlockSpec((PROJ_ROWS, d), lambda i: (i, 0))
    out_shapes = [jax.ShapeDtypeStruct((t, d_norm), _BF16)]
    out_specs = [pl.BlockSpec((PROJ_ROWS, d_norm), lambda i: (i, 0))]
    if d_out > d_norm:
        out_shapes.append(jax.ShapeDtypeStruct((t, d_out - d_norm), _BF16))
        out_specs.append(pl.BlockSpec((PROJ_ROWS, d_out - d_norm), lambda i: (i, 0)))
    return pl.pallas_call(
        functools.partial(_head_proj_kernel, d_norm=d_norm),
        grid=(t // PROJ_ROWS,),
        in_specs=[row_spec, _resident((1, d)), _resident(w.shape), _resident((1, d_norm)),
                  _resident(gather.shape), _resident(spread.shape)],
        out_specs=out_specs,
        out_shape=out_shapes,
        compiler_params=pltpu.CompilerParams(
            dimension_semantics=("arbitrary",), vmem_limit_bytes=VMEM_LIMIT_BYTES),
        name="head_proj",
    )(h, gain.reshape(1, d), w.astype(_BF16), hg, gather, spread)


def _attn_kernel(q_ref, k_ref, v_ref, o_ref, acc_ref, dead_ref):
    ig = pl.program_id(2)

    src = lax.broadcasted_iota(jnp.int32, (Q_BLOCK, Q_BLOCK), 0)
    dst = lax.broadcasted_iota(jnp.int32, (Q_BLOCK, Q_BLOCK), 1)
    suffix_mat = (src > dst).astype(_BF16)

    t_idx = lax.broadcasted_iota(jnp.int32, (2 * Q_BLOCK, Q_BLOCK), 0) % Q_BLOCK
    s_idx = lax.broadcasted_iota(jnp.int32, (2 * Q_BLOCK, Q_BLOCK), 1)
    causal = s_idx < t_idx

    def stack_heads(x):
        zero = jnp.zeros_like(x)
        keep = lax.broadcasted_iota(jnp.int32, x.shape, 1) < HEAD_DIM
        return jnp.concatenate([jnp.where(keep, x, zero), jnp.where(keep, zero, x)], axis=0)

    def rows_of(parts):
        return parts[0] if len(parts) == 1 else jnp.concatenate(parts, axis=0)

    def load_kv(j):
        start = j * Q_BLOCK
        if not isinstance(start, int):
            start = pl.multiple_of(start, Q_BLOCK)
        rows = pl.ds(start, Q_BLOCK)
        return k_ref[0, rows, :], stack_heads(v_ref[0, rows, :])

    def score(q2, kj):
        return lax.dot_general(q2, kj, (((1,), (1,)), ((), ())), preferred_element_type=_F32)

    def softplus(z, mask):
        y = jnp.maximum(z, 0.0) + jnp.log(1.0 + jnp.exp2(jnp.abs(z) * (-LOG2_E)))
        return y if mask is None else jnp.where(mask, y, 0.0)

    def later_sums(y):
        return _dot(y.astype(_BF16), suffix_mat)

    def weights(z, y, later, dead, mask):
        drop = y + later
        if dead is not None:
            drop = drop + dead
        a = jnp.exp(z - drop)
        if mask is not None:
            a = jnp.where(mask, a, 0.0)
        a = a.astype(_BF16)
        rows = a.shape[0] // HEADS_PER_STEP
        return jnp.concatenate([a[:rows], a[rows:]], axis=1)

    def first_rows(x, rows):
        return x if rows == Q_BLOCK else jnp.concatenate(
            [x[:rows], x[Q_BLOCK:Q_BLOCK + rows]], axis=0)

    def add_first_rows(x, part, rows):
        if rows == Q_BLOCK:
            return x + part
        pieces = []
        for h in range(x.shape[0] // Q_BLOCK):
            pieces += [x[h * Q_BLOCK:h * Q_BLOCK + rows] + part[h * rows:(h + 1) * rows],
                       x[h * Q_BLOCK + rows:(h + 1) * Q_BLOCK]]
        return jnp.concatenate(pieces, axis=0)

    def visit_rows(back):
        return ATTN_LAST_VISIT_ROWS if back == ATTN_UNROLLED_VISITS - 1 else Q_BLOCK

    def split_rows(x, visits, rows_per_query):
        out, at = {}, 0
        for v in visits:
            n = visit_rows(v[1]) * rows_per_query
            out[v] = x[at:at + n]
            at += n
        return out

    def unrolled_visits(first_block):
        users = {}
        for g in range(ATTN_Q_BLOCKS):
            for back in range(ATTN_UNROLLED_VISITS):
                if isinstance(first_block, int) and first_block + g - back < 0:
                    continue
                users.setdefault(g - back, []).append((g, back))
        offsets = sorted(users, reverse=True)
        order = [v for e in offsets for v in users[e]]
        q = [q_ref[0, g * Q_BLOCK:(g + 1) * Q_BLOCK, :] for g in range(ATTN_Q_BLOCKS)]
        q2 = {v: stack_heads(q[v[0]][:visit_rows(v[1])]) for v in order}
        kv = {e: load_kv(first_block + e) for e in offsets}

        z = {}
        for e in offsets:
            zc = score(rows_of([q2[v] for v in users[e]]), kv[e][0])
            z.update(split_rows(zc, users[e], HEADS_PER_STEP))
        y = {v: softplus(z[v], causal if v[1] == 0 else None) for v in order}
        later = split_rows(later_sums(rows_of([y[v] for v in order])), order, HEADS_PER_STEP)
        total = {v: jnp.sum(y[v], axis=-1, keepdims=True) for v in order}

        a_wide, dead_end = {}, {}
        for g in range(ATTN_Q_BLOCKS):
            dead = None
            for back in range(ATTN_UNROLLED_VISITS):
                v = (g, back)
                if v in z:
                    rows = visit_rows(back)
                    a_wide[v] = weights(z[v], y[v], later[v],
                                        None if dead is None else first_rows(dead, rows),
                                        causal if back == 0 else None)
                    dead = total[v] if dead is None else add_first_rows(dead, total[v], rows)
            dead_end[g] = dead

        acc = {}
        for e in offsets:
            c = split_rows(_dot(rows_of([a_wide[v] for v in users[e]]), kv[e][1]), users[e], 1)
            for v in users[e]:
                g = v[0]
                acc[g] = c[v] if g not in acc else add_first_rows(acc[g], c[v], visit_rows(v[1]))

        most_alive = None
        for g in range(ATTN_Q_BLOCKS):
            acc_ref[g] = acc[g]
            dead_ref[g] = dead_end[g]
            g_min = jnp.min(dead_end[g])
            most_alive = g_min if most_alive is None else jnp.minimum(most_alive, g_min)
        return most_alive < DEAD_LOG

    def visit(q2, kv, dead, mask):
        z = score(q2, kv[0])
        y = softplus(z, mask)
        a = weights(z, y, later_sums(y), dead, mask)
        return _dot(a, kv[1]), dead + jnp.sum(y, axis=-1, keepdims=True)

    def remaining_visits(first_block):
        for g in range(ATTN_Q_BLOCKS):
            last_block = first_block + g - (ATTN_UNROLLED_VISITS - 1)
            if isinstance(last_block, int) and last_block < 0:
                continue
            q2 = stack_heads(q_ref[0, g * Q_BLOCK:(g + 1) * Q_BLOCK, :])

            def step(j, mask, g=g, q2=q2):
                contrib, dead = visit(q2, load_kv(j), dead_ref[g], mask)
                acc_ref[g] += contrib
                dead_ref[g] = dead
                return jnp.min(dead) < DEAD_LOG

            def cond(carry):
                j, alive = carry
                return jnp.logical_and(j >= 0, alive)

            def body(carry, step=step):
                j, _ = carry
                return j - 1, step(j, None)

            @pl.when(jnp.min(dead_ref[g]) < DEAD_LOG)
            def _(last_block=last_block, step=step, cond=cond, body=body):
                alive = step(last_block, t_idx >= ATTN_LAST_VISIT_ROWS)
                lax.while_loop(cond, body, (last_block - 1, alive))

    @pl.when(ig == 0)
    def _():
        pl.when(unrolled_visits(0))(lambda: remaining_visits(0))

    @pl.when(ig > 0)
    def _():
        first_block = ig * ATTN_Q_BLOCKS
        pl.when(unrolled_visits(first_block))(lambda: remaining_visits(first_block))

    for g in range(ATTN_Q_BLOCKS):
        o_ref[0, g * Q_BLOCK:(g + 1) * Q_BLOCK, :] = acc_ref[g].astype(o_ref.dtype)


def _attention(q3, k3, v3):
    b, s, d = q3.shape
    group_rows = ATTN_Q_BLOCKS * Q_BLOCK
    assert ATTN_Q_BLOCKS >= ATTN_UNROLLED_VISITS - 1
    assert s % group_rows == 0 and d % LANES == 0
    assert 0 < ATTN_LAST_VISIT_ROWS <= Q_BLOCK and ATTN_LAST_VISIT_ROWS % BF16_SUBLANES == 0
    q_spec = pl.BlockSpec((1, group_rows, LANES), lambda bi, hp, i: (bi, i, hp))
    kv_spec = pl.BlockSpec((1, s, LANES), lambda bi, hp, i: (bi, 0, hp))
    return pl.pallas_call(
        _attn_kernel,
        grid=(b, d // LANES, s // group_rows),
        in_specs=[q_spec, kv_spec, kv_spec],
        out_specs=q_spec,
        out_shape=jax.ShapeDtypeStruct(q3.shape, _BF16),
        scratch_shapes=[pltpu.VMEM((ATTN_Q_BLOCKS, Q_BLOCK, LANES), _F32),
                        pltpu.VMEM((ATTN_Q_BLOCKS, HEADS_PER_STEP * Q_BLOCK, 1), _F32)],
        compiler_params=pltpu.CompilerParams(
            dimension_semantics=("arbitrary", "arbitrary", "arbitrary"),
            vmem_limit_bytes=VMEM_LIMIT_BYTES),
        name="attn",
    )(q3, k3, v3)


def kernel(x, ffn1_norm, ffn1_w_in, ffn1_w_out, ffn2_norm, ffn2_w_in, ffn2_w_out, pool_norm,
           pool_w, pool_scale, kv_norm, w_kv, k_gain, attn_norm, w_q, q_gain, w_o):
    b, s, d = x.shape
    assert d == N_HEADS * HEAD_DIM
    h = x.reshape(b * s, d)

    h = _ffn(h, ffn1_norm[0], ffn1_w_in[0], ffn1_w_out[0])
    h = _pool(h.reshape(b, s, d), pool_norm[0], pool_w[0], pool_scale[0]).reshape(b * s, d)
    h = _ffn(h, ffn2_norm[0], ffn2_w_in[0], ffn2_w_out[0])

    k, v = _head_proj(h, kv_norm, w_kv, k_gain, d_norm=d)

    h = _ffn(h, ffn1_norm[1], ffn1_w_in[1], ffn1_w_out[1])
    (q,) = _head_proj(h, attn_norm[0], w_q[0], q_gain[0] * (HEAD_DIM ** -0.5), d_norm=d)
    o = _attention(q.reshape(b, s, d), k.reshape(b, s, d), v.reshape(b, s, d))
    h = _ffn(h, ffn2_norm[1], ffn2_w_in[1], ffn2_w_out[1],
             mixer_out=o.reshape(b * s, d), w_proj=w_o[0])
    return h.reshape(b, s, d)
```

```python
import functools

import jax
import jax.numpy as jnp
from jax import lax
from jax.experimental import pallas as pl
from jax.experimental.pallas import tpu as pltpu

N_HEADS = 16
HEAD_DIM = 64
POOL_WINDOWS = (2, 4, 8, 16)
MAX_WINDOW = max(POOL_WINDOWS)
Q_BLOCK = 128
RMS_EPS = 1e-6
FFN_RES_WEIGHT = 0.5

LANES = 128
BF16_SUBLANES = 16
MXU_DIM = 256
VMEM_LIMIT_BYTES = 56 * 1024 * 1024

FFN_ROWS = 1024
FFN_CHUNK = MXU_DIM
POOL_ROW_BLOCKS = 2
PROJ_ROWS = 1024
HEADS_PER_STEP = LANES // HEAD_DIM
ATTN_Q_BLOCKS = 8
ATTN_UNROLLED_VISITS = 3
ATTN_LAST_VISIT_ROWS = 32

DEAD_LOG = 105.0
LOG2_E = 1.4426950408889634

_BF16 = jnp.bfloat16
_F32 = jnp.float32


def _resident(shape):
    zeros = (0,) * len(shape)
    return pl.BlockSpec(shape, lambda *_: zeros, pipeline_mode=pl.Buffered(1))


def _rms_norm(x, gain):
    ms = jnp.mean(x * x, axis=-1, keepdims=True)
    return x * lax.rsqrt(ms + RMS_EPS) * gain


def _dot(a, b):
    return jnp.dot(a, b, preferred_element_type=_F32)


def _split_bf16(x):
    hi = x.astype(_BF16)
    lo = (x - hi.astype(_F32)).astype(_BF16)
    return hi, lo


def _ffn_update(x, g_ref, win_ref, wout_ref, o_ref, side_work=()):
    d_ff = wout_ref.shape[0]
    n_chunks = d_ff // FFN_CHUNK
    emit_after = {(k + 1) * n_chunks // (len(side_work) + 1) - 1: work
                  for k, work in enumerate(side_work)}
    xn = _rms_norm(x, g_ref[...]).astype(_BF16)
    acc = x * (1.0 / FFN_RES_WEIGHT)
    for c in range(n_chunks):
        lo = c * FFN_CHUNK
        gate = _dot(xn, win_ref[:, lo:lo + FFN_CHUNK])
        up = _dot(xn, win_ref[:, d_ff + lo:d_ff + lo + FFN_CHUNK])
        act = (gate * jax.nn.sigmoid(gate) * up).astype(_BF16)
        acc = acc + _dot(act, wout_ref[lo:lo + FFN_CHUNK, :])
        if c in emit_after:
            emit_after[c]()
    o_ref[...] = FFN_RES_WEIGHT * acc


def _pool_pieces(x_ref, halo_ref, first_of_seq, g_ref, w_ref, scale_ref, o_ref):
    group_w = w_ref.shape[1]
    block = x_ref.shape[0] // POOL_ROW_BLOCKS
    shared = {}

    def normed(rb):
        if rb not in shared:
            gain = g_ref[...]
            if rb == 0:
                before = jnp.where(first_of_seq, 0.0, _rms_norm(halo_ref[...], gain))
                shared[rb] = jnp.concatenate([before, _rms_norm(x_ref[:block], gain)], axis=0)
            else:
                shared[rb] = _rms_norm(x_ref[rb * block - MAX_WINDOW:(rb + 1) * block], gain)
        return shared[rb]

    def piece(rb, g, w):
        ext = normed(rb)
        cols = slice(g * group_w, (g + 1) * group_w)
        rows = slice(rb * block, (rb + 1) * block)
        s = ext[:, cols]
        shift = 1
        while shift < w:
            s = s + pltpu.roll(s, shift, axis=0)
            shift *= 2
        if rb == 0:
            row = lax.broadcasted_iota(jnp.int32, (block, 1), 0)
            cnt = jnp.where(first_of_seq, jnp.minimum(row + 1, w), w)
            mean = s[MAX_WINDOW:] * (1.0 / cnt.astype(_F32))
        else:
            mean = s[MAX_WINDOW:] * (1.0 / w)
        y = _dot((mean - ext[MAX_WINDOW:, cols]).astype(_BF16), w_ref[g])
        o_ref[rows, cols] = x_ref[rows, cols] + y * scale_ref[:, cols]

    return [functools.partial(piece, rb, g, w)
            for rb in range(POOL_ROW_BLOCKS) for g, w in enumerate(POOL_WINDOWS)]


def _ffn_kernel(x_ref, g_ref, win_ref, wout_ref, o_ref):
    _ffn_update(x_ref[...], g_ref, win_ref, wout_ref, o_ref)


def _proj_ffn_kernel(x_ref, a_ref, wp_ref, g_ref, win_ref, wout_ref, o_ref):
    a = jnp.concatenate([a_ref[c] for c in range(a_ref.shape[0])], axis=1)
    x = x_ref[...] + _dot(a, wp_ref[...])
    _ffn_update(x, g_ref, win_ref, wout_ref, o_ref)


def _pool_ffn_kernel(x_ref, halo_ref, pg_ref, pw_ref, ps_ref, g_ref, win_ref, wout_ref, o_ref,
                     pooled_ref, *, tiles_per_seq):
    i = pl.program_id(0)
    tile = jnp.minimum(i, pl.num_programs(0) - 2)
    first_of_seq = tile % tiles_per_seq == 0

    def pool_pieces():
        return _pool_pieces(x_ref, halo_ref, first_of_seq, pg_ref, pw_ref, ps_ref, pooled_ref)

    @pl.when(i == 0)
    def _():
        for piece in pool_pieces():
            piece()

    @pl.when(i > 0)
    def _():
        _ffn_update(pooled_ref[...], g_ref, win_ref, wout_ref, o_ref, side_work=pool_pieces())


def _ffn_call(body, name, t, d, steps, row_maps, specs, args, gain, w_in, w_out, scratch=()):
    d_ff = w_out.shape[0]
    assert t % FFN_ROWS == 0 and d_ff % FFN_CHUNK == 0 and w_in.shape == (d, 2 * d_ff)
    ffn_specs = [_resident((1, d)), _resident(w_in.shape), _resident(w_out.shape)]
    return pl.pallas_call(
        body,
        grid=(steps,),
        in_specs=specs + ffn_specs,
        out_specs=pl.BlockSpec((FFN_ROWS, d), row_maps),
        out_shape=jax.ShapeDtypeStruct((t, d), _F32),
        scratch_shapes=list(scratch),
        compiler_params=pltpu.CompilerParams(
            dimension_semantics=("arbitrary",), vmem_limit_bytes=VMEM_LIMIT_BYTES),
        name=name,
    )(*args, gain.reshape(1, d), w_in.astype(_BF16), w_out.astype(_BF16))


def _ffn(h, gain, w_in, w_out):
    t, d = h.shape
    rows = lambda i: (i, 0)
    return _ffn_call(_ffn_kernel, "ffn", t, d, t // FFN_ROWS, rows,
                     [pl.BlockSpec((FFN_ROWS, d), rows)], (h,), gain, w_in, w_out)


def _proj_ffn(h, mixer_out, w_proj, gain, w_in, w_out):
    t, d = h.shape
    blocks = mixer_out.shape[0]
    assert mixer_out.shape == (blocks, t, LANES) and w_proj.shape == (blocks * LANES, d)
    rows = lambda i: (i, 0)
    specs = [pl.BlockSpec((FFN_ROWS, d), rows),
             pl.BlockSpec((blocks, FFN_ROWS, LANES), lambda i: (0, i, 0)),
             _resident(w_proj.shape)]
    return _ffn_call(_proj_ffn_kernel, "proj_ffn", t, d, t // FFN_ROWS, rows, specs,
                     (h, mixer_out, w_proj.astype(_BF16)), gain, w_in, w_out)


def _pool_ffn(h, seq_len, pool_gain, pool_w, pool_scale, gain, w_in, w_out):
    t, d = h.shape
    assert seq_len % FFN_ROWS == 0 and FFN_ROWS % MAX_WINDOW == 0 and t % seq_len == 0
    tiles = t // FFN_ROWS
    halo_blocks = FFN_ROWS // MAX_WINDOW
    pooled_tile = lambda i: jnp.minimum(i, tiles - 1)
    specs = [pl.BlockSpec((FFN_ROWS, d), lambda i: (pooled_tile(i), 0)),
             pl.BlockSpec((MAX_WINDOW, d),
                          lambda i: (jnp.maximum(pooled_tile(i) * halo_blocks - 1, 0), 0)),
             _resident((1, d)), _resident(pool_w.shape), _resident((1, d))]
    args = (h, h, pool_gain.reshape(1, d), pool_w.astype(_BF16), pool_scale.reshape(1, d))
    return _ffn_call(
        functools.partial(_pool_ffn_kernel, tiles_per_seq=seq_len // FFN_ROWS), "pool_ffn",
        t, d, tiles + 1, lambda i: (jnp.maximum(i - 1, 0), 0), specs, args, gain, w_in, w_out,
        scratch=[pltpu.VMEM((FFN_ROWS, d), _F32)])


def _head_sum_matrices(d):
    head_of_col = jnp.arange(d) // HEAD_DIM
    gather = (head_of_col[:, None] == jnp.arange(LANES)[None, :]).astype(_BF16)
    return gather, jnp.concatenate([gather.T, gather.T], axis=0)


def _head_proj_kernel(x_ref, g_ref, w_ref, hg_ref, gather_ref, spread_ref, *o_refs, d_norm):
    xn = _rms_norm(x_ref[...], g_ref[...]).astype(_BF16)
    y = _dot(xn, w_ref[...])
    yn = y[:, :d_norm]
    head_ss = _dot((yn * yn).astype(_BF16), gather_ref[...])
    r = lax.rsqrt(head_ss * (1.0 / HEAD_DIM) + RMS_EPS)
    r_hi, r_lo = _split_bf16(r)
    r_full = _dot(jnp.concatenate([r_hi, r_lo], axis=1), spread_ref[...])
    outs = [yn * r_full * hg_ref[...]]
    if len(o_refs) > 1:
        outs.append(y[:, d_norm:])
    for o_ref, out in zip(o_refs, outs):
        for c in range(o_ref.shape[0]):
            o_ref[c] = out[:, c * LANES:(c + 1) * LANES].astype(o_ref.dtype)


def _head_proj(h, gain, w, head_gain, d_norm):
    t, d = h.shape
    d_out = w.shape[1]
    assert t % PROJ_ROWS == 0 and d_norm % HEAD_DIM == 0 and d_norm // HEAD_DIM <= LANES
    assert d_norm % LANES == 0 and (d_out - d_norm) % LANES == 0
    gather, spread = _head_sum_matrices(d_norm)
    hg = jnp.tile(head_gain, d_norm // HEAD_DIM).reshape(1, d_norm)
    row_spec = pl.BlockSpec((PROJ_ROWS, d), lambda i: (i, 0))
    widths = [d_norm] + ([d_out - d_norm] if d_out > d_norm else [])
    out_shapes = [jax.ShapeDtypeStruct((n // LANES, t, LANES), _BF16) for n in widths]
    out_specs = [pl.BlockSpec((n // LANES, PROJ_ROWS, LANES), lambda i: (0, i, 0)) for n in widths]
    return pl.pallas_call(
        functools.partial(_head_proj_kernel, d_norm=d_norm),
        grid=(t // PROJ_ROWS,),
        in_specs=[row_spec, _resident((1, d)), _resident(w.shape), _resident((1, d_norm)),
                  _resident(gather.shape), _resident(spread.shape)],
        out_specs=out_specs,
        out_shape=out_shapes,
        compiler_params=pltpu.CompilerParams(
            dimension_semantics=("arbitrary",), vmem_limit_bytes=VMEM_LIMIT_BYTES),
        name="head_proj",
    )(h, gain.reshape(1, d), w.astype(_BF16), hg, gather, spread)


def _attn_kernel(q_ref, k_ref, v_ref, o_ref, acc_ref, dead_ref):
    ig = pl.program_id(2)

    src = lax.broadcasted_iota(jnp.int32, (Q_BLOCK, Q_BLOCK), 0)
    dst = lax.broadcasted_iota(jnp.int32, (Q_BLOCK, Q_BLOCK), 1)
    suffix_mat = (src > dst).astype(_BF16)

    t_idx = lax.broadcasted_iota(jnp.int32, (2 * Q_BLOCK, Q_BLOCK), 0) % Q_BLOCK
    s_idx = lax.broadcasted_iota(jnp.int32, (2 * Q_BLOCK, Q_BLOCK), 1)
    causal = s_idx < t_idx

    def stack_heads(x):
        zero = jnp.zeros_like(x)
        keep = lax.broadcasted_iota(jnp.int32, x.shape, 1) < HEAD_DIM
        return jnp.concatenate([jnp.where(keep, x, zero), jnp.where(keep, zero, x)], axis=0)

    def rows_of(parts):
        return parts[0] if len(parts) == 1 else jnp.concatenate(parts, axis=0)

    def load_kv(j):
        start = j * Q_BLOCK
        if not isinstance(start, int):
            start = pl.multiple_of(start, Q_BLOCK)
        rows = pl.ds(start, Q_BLOCK)
        return k_ref[rows, :], stack_heads(v_ref[rows, :])

    def score(q2, kj):
        return lax.dot_general(q2, kj, (((1,), (1,)), ((), ())), preferred_element_type=_F32)

    def softplus(z, mask):
        y = jnp.maximum(z, 0.0) + jnp.log(1.0 + jnp.exp2(jnp.abs(z) * (-LOG2_E)))
        return y if mask is None else jnp.where(mask, y, 0.0)

    def later_sums(y):
        return _dot(y.astype(_BF16), suffix_mat)

    def weights(z, y, later, dead, mask):
        drop = y + later
        if dead is not None:
            drop = drop + dead
        a = jnp.exp(z - drop)
        if mask is not None:
            a = jnp.where(mask, a, 0.0)
        a = a.astype(_BF16)
        rows = a.shape[0] // HEADS_PER_STEP
        return jnp.concatenate([a[:rows], a[rows:]], axis=1)

    def first_rows(x, rows):
        return x if rows == Q_BLOCK else jnp.concatenate(
            [x[:rows], x[Q_BLOCK:Q_BLOCK + rows]], axis=0)

    def add_first_rows(x, part, rows):
        if rows == Q_BLOCK:
            return x + part
        pieces = []
        for h in range(x.shape[0] // Q_BLOCK):
            pieces += [x[h * Q_BLOCK:h * Q_BLOCK + rows] + part[h * rows:(h + 1) * rows],
                       x[h * Q_BLOCK + rows:(h + 1) * Q_BLOCK]]
        return jnp.concatenate(pieces, axis=0)

    def visit_rows(back):
        return ATTN_LAST_VISIT_ROWS if back == ATTN_UNROLLED_VISITS - 1 else Q_BLOCK

    def split_rows(x, visits, rows_per_query):
        out, at = {}, 0
        for v in visits:
            n = visit_rows(v[1]) * rows_per_query
            out[v] = x[at:at + n]
            at += n
        return out

    def unrolled_visits(first_block):
        users = {}
        for g in range(ATTN_Q_BLOCKS):
            for back in range(ATTN_UNROLLED_VISITS):
                if isinstance(first_block, int) and first_block + g - back < 0:
                    continue
                users.setdefault(g - back, []).append((g, back))
        offsets = sorted(users, reverse=True)
        order = [v for e in offsets for v in users[e]]
        q = [q_ref[g * Q_BLOCK:(g + 1) * Q_BLOCK, :] for g in range(ATTN_Q_BLOCKS)]
        q2 = {v: stack_heads(q[v[0]][:visit_rows(v[1])]) for v in order}
        kv = {e: load_kv(first_block + e) for e in offsets}

        z = {}
        for e in offsets:
            zc = score(rows_of([q2[v] for v in users[e]]), kv[e][0])
            z.update(split_rows(zc, users[e], HEADS_PER_STEP))
        y = {v: softplus(z[v], causal if v[1] == 0 else None) for v in order}
        later = split_rows(later_sums(rows_of([y[v] for v in order])), order, HEADS_PER_STEP)
        total = {v: jnp.sum(y[v], axis=-1, keepdims=True) for v in order}

        a_wide, dead_end = {}, {}
        for g in range(ATTN_Q_BLOCKS):
            dead = None
            for back in range(ATTN_UNROLLED_VISITS):
                v = (g, back)
                if v in z:
                    rows = visit_rows(back)
                    a_wide[v] = weights(z[v], y[v], later[v],
                                        None if dead is None else first_rows(dead, rows),
                                        causal if back == 0 else None)
                    dead = total[v] if dead is None else add_first_rows(dead, total[v], rows)
            dead_end[g] = dead

        acc = {}
        for e in offsets:
            c = split_rows(_dot(rows_of([a_wide[v] for v in users[e]]), kv[e][1]), users[e], 1)
            for v in users[e]:
                g = v[0]
                acc[g] = c[v] if g not in acc else add_first_rows(acc[g], c[v], visit_rows(v[1]))

        most_alive = None
        for g in range(ATTN_Q_BLOCKS):
            acc_ref[g] = acc[g]
            dead_ref[g] = dead_end[g]
            g_min = jnp.min(dead_end[g])
            most_alive = g_min if most_alive is None else jnp.minimum(most_alive, g_min)
        return most_alive < DEAD_LOG

    def visit(q2, kv, dead, mask):
        z = score(q2, kv[0])
        y = softplus(z, mask)
        a = weights(z, y, later_sums(y), dead, mask)
        return _dot(a, kv[1]), dead + jnp.sum(y, axis=-1, keepdims=True)

    def remaining_visits(first_block):
        for g in range(ATTN_Q_BLOCKS):
            last_block = first_block + g - (ATTN_UNROLLED_VISITS - 1)
            if isinstance(last_block, int) and last_block < 0:
                continue
            q2 = stack_heads(q_ref[g * Q_BLOCK:(g + 1) * Q_BLOCK, :])

            def step(j, mask, g=g, q2=q2):
                contrib, dead = visit(q2, load_kv(j), dead_ref[g], mask)
                acc_ref[g] += contrib
                dead_ref[g] = dead
                return jnp.min(dead) < DEAD_LOG

            def cond(carry):
                j, alive = carry
                return jnp.logical_and(j >= 0, alive)

            def body(carry, step=step):
                j, _ = carry
                return j - 1, step(j, None)

            @pl.when(jnp.min(dead_ref[g]) < DEAD_LOG)
            def _(last_block=last_block, step=step, cond=cond, body=body):
                alive = step(last_block, t_idx >= ATTN_LAST_VISIT_ROWS)
                lax.while_loop(cond, body, (last_block - 1, alive))

    @pl.when(ig == 0)
    def _():
        pl.when(unrolled_visits(0))(lambda: remaining_visits(0))

    @pl.when(ig > 0)
    def _():
        first_block = ig * ATTN_Q_BLOCKS
        pl.when(unrolled_visits(first_block))(lambda: remaining_visits(first_block))

    for g in range(ATTN_Q_BLOCKS):
        o_ref[g * Q_BLOCK:(g + 1) * Q_BLOCK, :] = acc_ref[g].astype(o_ref.dtype)


def _attention(q, k, v):
    blocks, b, s, _ = q.shape
    group_rows = ATTN_Q_BLOCKS * Q_BLOCK
    assert ATTN_Q_BLOCKS >= ATTN_UNROLLED_VISITS - 1
    assert s % group_rows == 0 and q.shape[-1] == LANES
    assert 0 < ATTN_LAST_VISIT_ROWS <= Q_BLOCK and ATTN_LAST_VISIT_ROWS % BF16_SUBLANES == 0
    q_spec = pl.BlockSpec((None, None, group_rows, LANES), lambda bi, c, i: (c, bi, i, 0))
    kv_spec = pl.BlockSpec((None, None, s, LANES), lambda bi, c, i: (c, bi, 0, 0))
    return pl.pallas_call(
        _attn_kernel,
        grid=(b, blocks, s // group_rows),
        in_specs=[q_spec, kv_spec, kv_spec],
        out_specs=q_spec,
        out_shape=jax.ShapeDtypeStruct(q.shape, _BF16),
        scratch_shapes=[pltpu.VMEM((ATTN_Q_BLOCKS, Q_BLOCK, LANES), _F32),
                        pltpu.VMEM((ATTN_Q_BLOCKS, HEADS_PER_STEP * Q_BLOCK, 1), _F32)],
        compiler_params=pltpu.CompilerParams(
            dimension_semantics=("arbitrary", "arbitrary", "arbitrary"),
            vmem_limit_bytes=VMEM_LIMIT_BYTES),
        name="attn",
    )(q, k, v)


def kernel(x, ffn1_norm, ffn1_w_in, ffn1_w_out, ffn2_norm, ffn2_w_in, ffn2_w_out, pool_norm,
           pool_w, pool_scale, kv_norm, w_kv, k_gain, attn_norm, w_q, q_gain, w_o):
    b, s, d = x.shape
    assert d == N_HEADS * HEAD_DIM
    h = x.reshape(b * s, d)

    h = _ffn(h, ffn1_norm[0], ffn1_w_in[0], ffn1_w_out[0])
    h = _pool_ffn(h, s, pool_norm[0], pool_w[0], pool_scale[0],
                  ffn2_norm[0], ffn2_w_in[0], ffn2_w_out[0])

    k, v = _head_proj(h, kv_norm, w_kv, k_gain, d_norm=d)

    h = _ffn(h, ffn1_norm[1], ffn1_w_in[1], ffn1_w_out[1])
    (q,) = _head_proj(h, attn_norm[0], w_q[0], q_gain[0] * (HEAD_DIM ** -0.5), d_norm=d)
    per_batch = lambda a: a.reshape(a.shape[0], b, s, LANES)
    o = _attention(per_batch(q), per_batch(k), per_batch(v))
    h = _proj_ffn(h, o.reshape(o.shape[0], b * s, LANES), w_o[0],
                  ffn2_norm[1], ffn2_w_in[1], ffn2_w_out[1])
    return h.reshape(b, s, d)
```

```python
import functools

import jax
import jax.numpy as jnp
from jax import lax
from jax.experimental import pallas as pl
from jax.experimental.pallas import tpu as pltpu

N_HEADS = 16
HEAD_DIM = 64
POOL_WINDOWS = (2, 4, 8, 16)
MAX_WINDOW = max(POOL_WINDOWS)
Q_BLOCK = 128
RMS_EPS = 1e-6
FFN_RES_WEIGHT = 0.5

LANES = 128
BF16_SUBLANES = 16
MXU_DIM = 256
VMEM_LIMIT_BYTES = 56 * 1024 * 1024

FFN_ROWS = 1024
FFN_CHUNK = MXU_DIM
POOL_ROW_BLOCKS = 2
PROJ_ROWS = 1024
HEADS_PER_STEP = LANES // HEAD_DIM
ATTN_Q_BLOCKS = 16
ATTN_UNROLLED_VISITS = 3
ATTN_LAST_VISIT_ROWS = 32

DEAD_LOG = 105.0
LOG2_E = 1.4426950408889634

_BF16 = jnp.bfloat16
_F32 = jnp.float32


def _resident(shape):
    zeros = (0,) * len(shape)
    return pl.BlockSpec(shape, lambda *_: zeros, pipeline_mode=pl.Buffered(1))


def _rms_norm(x, gain):
    ms = jnp.mean(x * x, axis=-1, keepdims=True)
    return x * lax.rsqrt(ms + RMS_EPS) * gain


def _dot(a, b):
    return jnp.dot(a, b, preferred_element_type=_F32)


def _ffn_update(x, g_ref, win_ref, wout_ref, o_ref, side_work=()):
    d_ff = wout_ref.shape[0]
    n_chunks = d_ff // FFN_CHUNK
    emit_after = {(k + 1) * n_chunks // (len(side_work) + 1) - 1: work
                  for k, work in enumerate(side_work)}
    xn = _rms_norm(x, g_ref[...]).astype(_BF16)
    acc = x * (1.0 / FFN_RES_WEIGHT)
    for c in range(n_chunks):
        lo = c * FFN_CHUNK
        gate = _dot(xn, win_ref[:, lo:lo + FFN_CHUNK])
        up = _dot(xn, win_ref[:, d_ff + lo:d_ff + lo + FFN_CHUNK])
        act = (gate * jax.nn.sigmoid(gate) * up).astype(_BF16)
        acc = acc + _dot(act, wout_ref[lo:lo + FFN_CHUNK, :])
        if c in emit_after:
            emit_after[c]()
    o_ref[...] = FFN_RES_WEIGHT * acc


def _pool_pieces(x_ref, halo_ref, first_of_seq, g_ref, w_ref, scale_ref, o_ref):
    group_w = w_ref.shape[1]
    block = x_ref.shape[0] // POOL_ROW_BLOCKS
    shared = {}

    def normed(rb):
        if rb not in shared:
            gain = g_ref[...]
            if rb == 0:
                before = jnp.where(first_of_seq, 0.0, _rms_norm(halo_ref[...], gain))
                shared[rb] = jnp.concatenate([before, _rms_norm(x_ref[:block], gain)], axis=0)
            else:
                shared[rb] = _rms_norm(x_ref[rb * block - MAX_WINDOW:(rb + 1) * block], gain)
        return shared[rb]

    def piece(rb, g, w):
        ext = normed(rb)
        cols = slice(g * group_w, (g + 1) * group_w)
        rows = slice(rb * block, (rb + 1) * block)
        s = ext[:, cols]
        shift = 1
        while shift < w:
            s = s + pltpu.roll(s, shift, axis=0)
            shift *= 2
        if rb == 0:
            row = lax.broadcasted_iota(jnp.int32, (block, 1), 0)
            cnt = jnp.where(first_of_seq, jnp.minimum(row + 1, w), w)
            mean = s[MAX_WINDOW:] * (1.0 / cnt.astype(_F32))
        else:
            mean = s[MAX_WINDOW:] * (1.0 / w)
        y = _dot((mean - ext[MAX_WINDOW:, cols]).astype(_BF16), w_ref[g])
        o_ref[rows, cols] = x_ref[rows, cols] + y * scale_ref[:, cols]

    return [functools.partial(piece, rb, g, w)
            for rb in range(POOL_ROW_BLOCKS) for g, w in enumerate(POOL_WINDOWS)]


def _ffn_kernel(x_ref, g_ref, win_ref, wout_ref, o_ref):
    _ffn_update(x_ref[...], g_ref, win_ref, wout_ref, o_ref)


def _proj_ffn_kernel(x_ref, a_ref, wp_ref, g_ref, win_ref, wout_ref, o_ref):
    a = jnp.concatenate([a_ref[c] for c in range(a_ref.shape[0])], axis=1)
    x = x_ref[...] + _dot(a, wp_ref[...])
    _ffn_update(x, g_ref, win_ref, wout_ref, o_ref)


def _pool_ffn_kernel(x_ref, halo_ref, pg_ref, pw_ref, ps_ref, g_ref, win_ref, wout_ref, o_ref,
                     pooled_ref, *, tiles_per_seq):
    i = pl.program_id(0)
    tile = jnp.minimum(i, pl.num_programs(0) - 2)
    first_of_seq = tile % tiles_per_seq == 0

    def pool_pieces():
        return _pool_pieces(x_ref, halo_ref, first_of_seq, pg_ref, pw_ref, ps_ref, pooled_ref)

    @pl.when(i == 0)
    def _():
        for piece in pool_pieces():
            piece()

    @pl.when(i > 0)
    def _():
        _ffn_update(pooled_ref[...], g_ref, win_ref, wout_ref, o_ref, side_work=pool_pieces())


def _ffn_call(body, name, t, d, steps, row_maps, specs, args, gain, w_in, w_out, scratch=()):
    d_ff = w_out.shape[0]
    assert t % FFN_ROWS == 0 and d_ff % FFN_CHUNK == 0 and w_in.shape == (d, 2 * d_ff)
    ffn_specs = [_resident((1, d)), _resident(w_in.shape), _resident(w_out.shape)]
    return pl.pallas_call(
        body,
        grid=(steps,),
        in_specs=specs + ffn_specs,
        out_specs=pl.BlockSpec((FFN_ROWS, d), row_maps),
        out_shape=jax.ShapeDtypeStruct((t, d), _F32),
        scratch_shapes=list(scratch),
        compiler_params=pltpu.CompilerParams(
            dimension_semantics=("arbitrary",), vmem_limit_bytes=VMEM_LIMIT_BYTES),
        name=name,
    )(*args, gain.reshape(1, d), w_in.astype(_BF16), w_out.astype(_BF16))


def _ffn(h, gain, w_in, w_out):
    t, d = h.shape
    rows = lambda i: (i, 0)
    return _ffn_call(_ffn_kernel, "ffn", t, d, t // FFN_ROWS, rows,
                     [pl.BlockSpec((FFN_ROWS, d), rows)], (h,), gain, w_in, w_out)


def _proj_ffn(h, mixer_out, w_proj, gain, w_in, w_out):
    t, d = h.shape
    blocks = mixer_out.shape[0]
    assert mixer_out.shape == (blocks, t, LANES) and w_proj.shape == (blocks * LANES, d)
    rows = lambda i: (i, 0)
    specs = [pl.BlockSpec((FFN_ROWS, d), rows),
             pl.BlockSpec((blocks, FFN_ROWS, LANES), lambda i: (0, i, 0)),
             _resident(w_proj.shape)]
    return _ffn_call(_proj_ffn_kernel, "proj_ffn", t, d, t // FFN_ROWS, rows, specs,
                     (h, mixer_out, w_proj.astype(_BF16)), gain, w_in, w_out)


def _pool_ffn(h, seq_len, pool_gain, pool_w, pool_scale, gain, w_in, w_out):
    t, d = h.shape
    assert seq_len % FFN_ROWS == 0 and FFN_ROWS % MAX_WINDOW == 0 and t % seq_len == 0
    tiles = t // FFN_ROWS
    halo_blocks = FFN_ROWS // MAX_WINDOW
    pooled_tile = lambda i: jnp.minimum(i, tiles - 1)
    specs = [pl.BlockSpec((FFN_ROWS, d), lambda i: (pooled_tile(i), 0)),
             pl.BlockSpec((MAX_WINDOW, d),
                          lambda i: (jnp.maximum(pooled_tile(i) * halo_blocks - 1, 0), 0)),
             _resident((1, d)), _resident(pool_w.shape), _resident((1, d))]
    args = (h, h, pool_gain.reshape(1, d), pool_w.astype(_BF16), pool_scale.reshape(1, d))
    return _ffn_call(
        functools.partial(_pool_ffn_kernel, tiles_per_seq=seq_len // FFN_ROWS), "pool_ffn",
        t, d, tiles + 1, lambda i: (jnp.maximum(i - 1, 0), 0), specs, args, gain, w_in, w_out,
        scratch=[pltpu.VMEM((FFN_ROWS, d), _F32)])


def _head_proj_kernel(x_ref, g_ref, w_ref, hg_ref, *o_refs, d_norm):
    xn = _rms_norm(x_ref[...], g_ref[...]).astype(_BF16)
    y = _dot(xn, w_ref[...])
    first_head = lax.broadcasted_iota(jnp.int32, (y.shape[0], LANES), 1) < HEAD_DIM
    for c in range(d_norm // LANES):
        cols = slice(c * LANES, (c + 1) * LANES)
        yc = y[:, cols]
        sq = yc * yc
        ms = [jnp.sum(jnp.where(keep, sq, 0.0), axis=-1, keepdims=True) * (1.0 / HEAD_DIM)
              for keep in (first_head, ~first_head)]
        r = jnp.where(first_head, lax.rsqrt(ms[0] + RMS_EPS), lax.rsqrt(ms[1] + RMS_EPS))
        o_refs[0][c] = (yc * r * hg_ref[:, cols]).astype(o_refs[0].dtype)
    if len(o_refs) > 1:
        for c in range(o_refs[1].shape[0]):
            o_refs[1][c] = y[:, d_norm + c * LANES:d_norm + (c + 1) * LANES].astype(
                o_refs[1].dtype)


def _head_proj(h, gain, w, head_gain, d_norm):
    t, d = h.shape
    d_out = w.shape[1]
    assert t % PROJ_ROWS == 0 and HEADS_PER_STEP == 2
    assert d_norm % LANES == 0 and (d_out - d_norm) % LANES == 0
    hg = jnp.tile(head_gain, d_norm // HEAD_DIM).reshape(1, d_norm)
    row_spec = pl.BlockSpec((PROJ_ROWS, d), lambda i: (i, 0))
    widths = [d_norm] + ([d_out - d_norm] if d_out > d_norm else [])
    out_shapes = [jax.ShapeDtypeStruct((n // LANES, t, LANES), _BF16) for n in widths]
    out_specs = [pl.BlockSpec((n // LANES, PROJ_ROWS, LANES), lambda i: (0, i, 0)) for n in widths]
    return pl.pallas_call(
        functools.partial(_head_proj_kernel, d_norm=d_norm),
        grid=(t // PROJ_ROWS,),
        in_specs=[row_spec, _resident((1, d)), _resident(w.shape), _resident((1, d_norm))],
        out_specs=out_specs,
        out_shape=out_shapes,
        compiler_params=pltpu.CompilerParams(
            dimension_semantics=("arbitrary",), vmem_limit_bytes=VMEM_LIMIT_BYTES),
        name="head_proj",
    )(h, gain.reshape(1, d), w.astype(_BF16), hg)


def _attn_kernel(q_ref, k_ref, v_ref, o_ref, acc_ref, dead_ref):
    ig = pl.program_id(2)

    src = lax.broadcasted_iota(jnp.int32, (Q_BLOCK, Q_BLOCK), 0)
    dst = lax.broadcasted_iota(jnp.int32, (Q_BLOCK, Q_BLOCK), 1)
    suffix_mat = (src > dst).astype(_BF16)

    t_idx = lax.broadcasted_iota(jnp.int32, (2 * Q_BLOCK, Q_BLOCK), 0) % Q_BLOCK
    s_idx = lax.broadcasted_iota(jnp.int32, (2 * Q_BLOCK, Q_BLOCK), 1)
    causal = s_idx < t_idx

    def stack_heads(x):
        zero = jnp.zeros_like(x)
        keep = lax.broadcasted_iota(jnp.int32, x.shape, 1) < HEAD_DIM
        return jnp.concatenate([jnp.where(keep, x, zero), jnp.where(keep, zero, x)], axis=0)

    def rows_of(parts):
        return parts[0] if len(parts) == 1 else jnp.concatenate(parts, axis=0)

    def load_kv(j):
        start = j * Q_BLOCK
        if not isinstance(start, int):
            start = pl.multiple_of(start, Q_BLOCK)
        rows = pl.ds(start, Q_BLOCK)
        return k_ref[rows, :], stack_heads(v_ref[rows, :])

    def score(q2, kj):
        return lax.dot_general(q2, kj, (((1,), (1,)), ((), ())), preferred_element_type=_F32)

    def softplus(z, mask):
        y = jnp.maximum(z, 0.0) + jnp.log(1.0 + jnp.exp2(jnp.abs(z) * (-LOG2_E)))
        return y if mask is None else jnp.where(mask, y, 0.0)

    def later_sums(y):
        return _dot(y.astype(_BF16), suffix_mat)

    def weights(z, y, later, dead, mask):
        drop = y + later
        if dead is not None:
            drop = drop + dead
        a = jnp.exp(z - drop)
        if mask is not None:
            a = jnp.where(mask, a, 0.0)
        a = a.astype(_BF16)
        rows = a.shape[0] // HEADS_PER_STEP
        return jnp.concatenate([a[:rows], a[rows:]], axis=1)

    def first_rows(x, rows):
        return x if rows == Q_BLOCK else jnp.concatenate(
            [x[:rows], x[Q_BLOCK:Q_BLOCK + rows]], axis=0)

    def add_first_rows(x, part, rows):
        if rows == Q_BLOCK:
            return x + part
        pieces = []
        for h in range(x.shape[0] // Q_BLOCK):
            pieces += [x[h * Q_BLOCK:h * Q_BLOCK + rows] + part[h * rows:(h + 1) * rows],
                       x[h * Q_BLOCK + rows:(h + 1) * Q_BLOCK]]
        return jnp.concatenate(pieces, axis=0)

    def visit_rows(back):
        return ATTN_LAST_VISIT_ROWS if back == ATTN_UNROLLED_VISITS - 1 else Q_BLOCK

    def split_rows(x, visits, rows_per_query):
        out, at = {}, 0
        for v in visits:
            n = visit_rows(v[1]) * rows_per_query
            out[v] = x[at:at + n]
            at += n
        return out

    def unrolled_visits(first_block):
        users = {}
        for g in range(ATTN_Q_BLOCKS):
            for back in range(ATTN_UNROLLED_VISITS):
                if isinstance(first_block, int) and first_block + g - back < 0:
                    continue
                users.setdefault(g - back, []).append((g, back))
        offsets = sorted(users, reverse=True)
        order = [v for e in offsets for v in users[e]]
        q = [q_ref[g * Q_BLOCK:(g + 1) * Q_BLOCK, :] for g in range(ATTN_Q_BLOCKS)]
        q2 = {v: stack_heads(q[v[0]][:visit_rows(v[1])]) for v in order}
        kv = {e: load_kv(first_block + e) for e in offsets}

        z = {}
        for e in offsets:
            zc = score(rows_of([q2[v] for v in users[e]]), kv[e][0])
            z.update(split_rows(zc, users[e], HEADS_PER_STEP))
        y = {v: softplus(z[v], causal if v[1] == 0 else None) for v in order}
        later = split_rows(later_sums(rows_of([y[v] for v in order])), order, HEADS_PER_STEP)
        total = {v: jnp.sum(y[v], axis=-1, keepdims=True) for v in order}

        a_wide, dead_end = {}, {}
        for g in range(ATTN_Q_BLOCKS):
            dead = None
            for back in range(ATTN_UNROLLED_VISITS):
                v = (g, back)
                if v in z:
                    rows = visit_rows(back)
                    a_wide[v] = weights(z[v], y[v], later[v],
                                        None if dead is None else first_rows(dead, rows),
                                        causal if back == 0 else None)
                    dead = total[v] if dead is None else add_first_rows(dead, total[v], rows)
            dead_end[g] = dead

        acc = {}
        for e in offsets:
            c = split_rows(_dot(rows_of([a_wide[v] for v in users[e]]), kv[e][1]), users[e], 1)
            for v in users[e]:
                g = v[0]
                acc[g] = c[v] if g not in acc else add_first_rows(acc[g], c[v], visit_rows(v[1]))

        most_alive = None
        for g in range(ATTN_Q_BLOCKS):
            acc_ref[g] = acc[g]
            dead_ref[g] = dead_end[g]
            g_min = jnp.min(dead_end[g])
            most_alive = g_min if most_alive is None else jnp.minimum(most_alive, g_min)
        return most_alive < DEAD_LOG

    def visit(q2, kv, dead, mask):
        z = score(q2, kv[0])
        y = softplus(z, mask)
        a = weights(z, y, later_sums(y), dead, mask)
        return _dot(a, kv[1]), dead + jnp.sum(y, axis=-1, keepdims=True)

    def remaining_visits(first_block):
        for g in range(ATTN_Q_BLOCKS):
            last_block = first_block + g - (ATTN_UNROLLED_VISITS - 1)
            if isinstance(last_block, int) and last_block < 0:
                continue
            q2 = stack_heads(q_ref[g * Q_BLOCK:(g + 1) * Q_BLOCK, :])

            def step(j, mask, g=g, q2=q2):
                contrib, dead = visit(q2, load_kv(j), dead_ref[g], mask)
                acc_ref[g] += contrib
                dead_ref[g] = dead
                return jnp.min(dead) < DEAD_LOG

            def cond(carry):
                j, alive = carry
                return jnp.logical_and(j >= 0, alive)

            def body(carry, step=step):
                j, _ = carry
                return j - 1, step(j, None)

            @pl.when(jnp.min(dead_ref[g]) < DEAD_LOG)
            def _(last_block=last_block, step=step, cond=cond, body=body):
                alive = step(last_block, t_idx >= ATTN_LAST_VISIT_ROWS)
                lax.while_loop(cond, body, (last_block - 1, alive))

    @pl.when(ig == 0)
    def _():
        pl.when(unrolled_visits(0))(lambda: remaining_visits(0))

    @pl.when(ig > 0)
    def _():
        first_block = ig * ATTN_Q_BLOCKS
        pl.when(unrolled_visits(first_block))(lambda: remaining_visits(first_block))

    for g in range(ATTN_Q_BLOCKS):
        o_ref[g * Q_BLOCK:(g + 1) * Q_BLOCK, :] = acc_ref[g].astype(o_ref.dtype)


def _attention(q, k, v):
    blocks, b, s, _ = q.shape
    group_rows = ATTN_Q_BLOCKS * Q_BLOCK
    assert ATTN_Q_BLOCKS >= ATTN_UNROLLED_VISITS - 1
    assert s % group_rows == 0 and q.shape[-1] == LANES
    assert 0 < ATTN_LAST_VISIT_ROWS <= Q_BLOCK and ATTN_LAST_VISIT_ROWS % BF16_SUBLANES == 0
    q_spec = pl.BlockSpec((None, None, group_rows, LANES), lambda bi, c, i: (c, bi, i, 0))
    kv_spec = pl.BlockSpec((None, None, s, LANES), lambda bi, c, i: (c, bi, 0, 0))
    return pl.pallas_call(
        _attn_kernel,
        grid=(b, blocks, s // group_rows),
        in_specs=[q_spec, kv_spec, kv_spec],
        out_specs=q_spec,
        out_shape=jax.ShapeDtypeStruct(q.shape, _BF16),
        scratch_shapes=[pltpu.VMEM((ATTN_Q_BLOCKS, Q_BLOCK, LANES), _F32),
                        pltpu.VMEM((ATTN_Q_BLOCKS, HEADS_PER_STEP * Q_BLOCK, 1), _F32)],
        compiler_params=pltpu.CompilerParams(
            dimension_semantics=("arbitrary", "arbitrary", "arbitrary"),
            vmem_limit_bytes=VMEM_LIMIT_BYTES),
        name="attn",
    )(q, k, v)


def kernel(x, ffn1_norm, ffn1_w_in, ffn1_w_out, ffn2_norm, ffn2_w_in, ffn2_w_out, pool_norm,
           pool_w, pool_scale, kv_norm, w_kv, k_gain, attn_norm, w_q, q_gain, w_o):
    b, s, d = x.shape
    assert d == N_HEADS * HEAD_DIM
    h = x.reshape(b * s, d)

    h = _ffn(h, ffn1_norm[0], ffn1_w_in[0], ffn1_w_out[0])
    h = _pool_ffn(h, s, pool_norm[0], pool_w[0], pool_scale[0],
                  ffn2_norm[0], ffn2_w_in[0], ffn2_w_out[0])

    k, v = _head_proj(h, kv_norm, w_kv, k_gain, d_norm=d)

    h = _ffn(h, ffn1_norm[1], ffn1_w_in[1], ffn1_w_out[1])
    (q,) = _head_proj(h, attn_norm[0], w_q[0], q_gain[0] * (HEAD_DIM ** -0.5), d_norm=d)
    per_batch = lambda a: a.reshape(a.shape[0], b, s, LANES)
    o = _attention(per_batch(q), per_batch(k), per_batch(v))
    h = _proj_ffn(h, o.reshape(o.shape[0], b * s, LANES), w_o[0],
                  ffn2_norm[1], ffn2_w_in[1], ffn2_w_out[1])
    return h.reshape(b, s, d)
```

```python
import functools

import jax
import jax.numpy as jnp
from jax import lax
from jax.experimental import pallas as pl
from jax.experimental.pallas import tpu as pltpu

N_HEADS = 16
HEAD_DIM = 64
POOL_WINDOWS = (2, 4, 8, 16)
MAX_WINDOW = max(POOL_WINDOWS)
Q_BLOCK = 128
RMS_EPS = 1e-6
FFN_RES_WEIGHT = 0.5

LANES = 128
BF16_SUBLANES = 16
MXU_DIM = 256
VMEM_LIMIT_BYTES = 56 * 1024 * 1024

FFN_ROWS = 1024
FFN_CHUNK = MXU_DIM
POOL_ROW_BLOCKS = 2
PROJ_ROWS = 1024
HEADS_PER_STEP = LANES // HEAD_DIM
ATTN_Q_BLOCKS = 16
ATTN_UNROLLED_VISITS = 3
ATTN_LAST_VISIT_ROWS = 32
ATTN_STAGE_LAG = 3

DEAD_LOG = 105.0
LOG2_E = 1.4426950408889634

_BF16 = jnp.bfloat16
_F32 = jnp.float32


def _resident(shape):
    zeros = (0,) * len(shape)
    return pl.BlockSpec(shape, lambda *_: zeros, pipeline_mode=pl.Buffered(1))


def _rms_norm(x, gain):
    ms = jnp.mean(x * x, axis=-1, keepdims=True)
    return x * lax.rsqrt(ms + RMS_EPS) * gain


def _dot(a, b):
    return jnp.dot(a, b, preferred_element_type=_F32)


def _ffn_update(x, g_ref, win_ref, wout_ref, o_ref, side_work=()):
    d_ff = wout_ref.shape[0]
    n_chunks = d_ff // FFN_CHUNK
    emit_after = {(k + 1) * n_chunks // (len(side_work) + 1) - 1: work
                  for k, work in enumerate(side_work)}
    xn = _rms_norm(x, g_ref[...]).astype(_BF16)
    acc = x * (1.0 / FFN_RES_WEIGHT)
    for c in range(n_chunks):
        lo = c * FFN_CHUNK
        gate = _dot(xn, win_ref[:, lo:lo + FFN_CHUNK])
        up = _dot(xn, win_ref[:, d_ff + lo:d_ff + lo + FFN_CHUNK])
        act = (gate * jax.nn.sigmoid(gate) * up).astype(_BF16)
        acc = acc + _dot(act, wout_ref[lo:lo + FFN_CHUNK, :])
        if c in emit_after:
            emit_after[c]()
    o_ref[...] = FFN_RES_WEIGHT * acc


def _pool_pieces(x_ref, halo_ref, first_of_seq, g_ref, w_ref, scale_ref, o_ref):
    group_w = w_ref.shape[1]
    block = x_ref.shape[0] // POOL_ROW_BLOCKS
    shared = {}

    def normed(rb):
        if rb not in shared:
            gain = g_ref[...]
            if rb == 0:
                before = jnp.where(first_of_seq, 0.0, _rms_norm(halo_ref[...], gain))
                shared[rb] = jnp.concatenate([before, _rms_norm(x_ref[:block], gain)], axis=0)
            else:
                shared[rb] = _rms_norm(x_ref[rb * block - MAX_WINDOW:(rb + 1) * block], gain)
        return shared[rb]

    def piece(rb, g, w):
        ext = normed(rb)
        cols = slice(g * group_w, (g + 1) * group_w)
        rows = slice(rb * block, (rb + 1) * block)
        s = ext[:, cols]
        shift = 1
        while shift < w:
            s = s + pltpu.roll(s, shift, axis=0)
            shift *= 2
        if rb == 0:
            row = lax.broadcasted_iota(jnp.int32, (block, 1), 0)
            cnt = jnp.where(first_of_seq, jnp.minimum(row + 1, w), w)
            mean = s[MAX_WINDOW:] * (1.0 / cnt.astype(_F32))
        else:
            mean = s[MAX_WINDOW:] * (1.0 / w)
        y = _dot((mean - ext[MAX_WINDOW:, cols]).astype(_BF16), w_ref[g])
        o_ref[rows, cols] = x_ref[rows, cols] + y * scale_ref[:, cols]

    return [functools.partial(piece, rb, g, w)
            for rb in range(POOL_ROW_BLOCKS) for g, w in enumerate(POOL_WINDOWS)]


def _ffn_kernel(x_ref, g_ref, win_ref, wout_ref, o_ref):
    _ffn_update(x_ref[...], g_ref, win_ref, wout_ref, o_ref)


def _proj_ffn_kernel(x_ref, a_ref, wp_ref, g_ref, win_ref, wout_ref, o_ref):
    a = jnp.concatenate([a_ref[c] for c in range(a_ref.shape[0])], axis=1)
    x = x_ref[...] + _dot(a, wp_ref[...])
    _ffn_update(x, g_ref, win_ref, wout_ref, o_ref)


def _pool_ffn_kernel(x_ref, halo_ref, pg_ref, pw_ref, ps_ref, g_ref, win_ref, wout_ref, o_ref,
                     pooled_ref, *, tiles_per_seq):
    i = pl.program_id(0)
    tile = jnp.minimum(i, pl.num_programs(0) - 2)
    first_of_seq = tile % tiles_per_seq == 0

    def pool_pieces():
        return _pool_pieces(x_ref, halo_ref, first_of_seq, pg_ref, pw_ref, ps_ref, pooled_ref)

    @pl.when(i == 0)
    def _():
        for piece in pool_pieces():
            piece()

    @pl.when(i > 0)
    def _():
        _ffn_update(pooled_ref[...], g_ref, win_ref, wout_ref, o_ref, side_work=pool_pieces())


def _ffn_call(body, name, t, d, steps, row_maps, specs, args, gain, w_in, w_out, scratch=()):
    d_ff = w_out.shape[0]
    assert t % FFN_ROWS == 0 and d_ff % FFN_CHUNK == 0 and w_in.shape == (d, 2 * d_ff)
    ffn_specs = [_resident((1, d)), _resident(w_in.shape), _resident(w_out.shape)]
    return pl.pallas_call(
        body,
        grid=(steps,),
        in_specs=specs + ffn_specs,
        out_specs=pl.BlockSpec((FFN_ROWS, d), row_maps),
        out_shape=jax.ShapeDtypeStruct((t, d), _F32),
        scratch_shapes=list(scratch),
        compiler_params=pltpu.CompilerParams(
            dimension_semantics=("arbitrary",), vmem_limit_bytes=VMEM_LIMIT_BYTES),
        name=name,
    )(*args, gain.reshape(1, d), w_in.astype(_BF16), w_out.astype(_BF16))


def _ffn(h, gain, w_in, w_out):
    t, d = h.shape
    rows = lambda i: (i, 0)
    return _ffn_call(_ffn_kernel, "ffn", t, d, t // FFN_ROWS, rows,
                     [pl.BlockSpec((FFN_ROWS, d), rows)], (h,), gain, w_in, w_out)


def _proj_ffn(h, mixer_out, w_proj, gain, w_in, w_out):
    t, d = h.shape
    blocks = mixer_out.shape[0]
    assert mixer_out.shape == (blocks, t, LANES) and w_proj.shape == (blocks * LANES, d)
    rows = lambda i: (i, 0)
    specs = [pl.BlockSpec((FFN_ROWS, d), rows),
             pl.BlockSpec((blocks, FFN_ROWS, LANES), lambda i: (0, i, 0)),
             _resident(w_proj.shape)]
    return _ffn_call(_proj_ffn_kernel, "proj_ffn", t, d, t // FFN_ROWS, rows, specs,
                     (h, mixer_out, w_proj.astype(_BF16)), gain, w_in, w_out)


def _pool_ffn(h, seq_len, pool_gain, pool_w, pool_scale, gain, w_in, w_out):
    t, d = h.shape
    assert seq_len % FFN_ROWS == 0 and FFN_ROWS % MAX_WINDOW == 0 and t % seq_len == 0
    tiles = t // FFN_ROWS
    halo_blocks = FFN_ROWS // MAX_WINDOW
    pooled_tile = lambda i: jnp.minimum(i, tiles - 1)
    specs = [pl.BlockSpec((FFN_ROWS, d), lambda i: (pooled_tile(i), 0)),
             pl.BlockSpec((MAX_WINDOW, d),
                          lambda i: (jnp.maximum(pooled_tile(i) * halo_blocks - 1, 0), 0)),
             _resident((1, d)), _resident(pool_w.shape), _resident((1, d))]
    args = (h, h, pool_gain.reshape(1, d), pool_w.astype(_BF16), pool_scale.reshape(1, d))
    return _ffn_call(
        functools.partial(_pool_ffn_kernel, tiles_per_seq=seq_len // FFN_ROWS), "pool_ffn",
        t, d, tiles + 1, lambda i: (jnp.maximum(i - 1, 0), 0), specs, args, gain, w_in, w_out,
        scratch=[pltpu.VMEM((FFN_ROWS, d), _F32)])


def _head_proj_kernel(x_ref, g_ref, w_ref, hg_ref, *o_refs, d_norm):
    xn = _rms_norm(x_ref[...], g_ref[...]).astype(_BF16)
    y = _dot(xn, w_ref[...])
    first_head = lax.broadcasted_iota(jnp.int32, (y.shape[0], LANES), 1) < HEAD_DIM
    for c in range(d_norm // LANES):
        cols = slice(c * LANES, (c + 1) * LANES)
        yc = y[:, cols]
        sq = yc * yc
        ms = [jnp.sum(jnp.where(keep, sq, 0.0), axis=-1, keepdims=True) * (1.0 / HEAD_DIM)
              for keep in (first_head, ~first_head)]
        r = jnp.where(first_head, lax.rsqrt(ms[0] + RMS_EPS), lax.rsqrt(ms[1] + RMS_EPS))
        o_refs[0][c] = (yc * r * hg_ref[:, cols]).astype(o_refs[0].dtype)
    if len(o_refs) > 1:
        for c in range(o_refs[1].shape[0]):
            o_refs[1][c] = y[:, d_norm + c * LANES:d_norm + (c + 1) * LANES].astype(
                o_refs[1].dtype)


def _head_proj(h, gain, w, head_gain, d_norm):
    t, d = h.shape
    d_out = w.shape[1]
    assert t % PROJ_ROWS == 0 and HEADS_PER_STEP == 2
    assert d_norm % LANES == 0 and (d_out - d_norm) % LANES == 0
    hg = jnp.tile(head_gain, d_norm // HEAD_DIM).reshape(1, d_norm)
    row_spec = pl.BlockSpec((PROJ_ROWS, d), lambda i: (i, 0))
    widths = [d_norm] + ([d_out - d_norm] if d_out > d_norm else [])
    out_shapes = [jax.ShapeDtypeStruct((n // LANES, t, LANES), _BF16) for n in widths]
    out_specs = [pl.BlockSpec((n // LANES, PROJ_ROWS, LANES), lambda i: (0, i, 0)) for n in widths]
    return pl.pallas_call(
        functools.partial(_head_proj_kernel, d_norm=d_norm),
        grid=(t // PROJ_ROWS,),
        in_specs=[row_spec, _resident((1, d)), _resident(w.shape), _resident((1, d_norm))],
        out_specs=out_specs,
        out_shape=out_shapes,
        compiler_params=pltpu.CompilerParams(
            dimension_semantics=("arbitrary",), vmem_limit_bytes=VMEM_LIMIT_BYTES),
        name="head_proj",
    )(h, gain.reshape(1, d), w.astype(_BF16), hg)


def _attn_kernel(q_ref, k_ref, v_ref, o_ref, acc_ref, dead_ref):
    ig = pl.program_id(2)

    src = lax.broadcasted_iota(jnp.int32, (Q_BLOCK, Q_BLOCK), 0)
    dst = lax.broadcasted_iota(jnp.int32, (Q_BLOCK, Q_BLOCK), 1)
    suffix_mat = (src > dst).astype(_BF16)

    t_idx = lax.broadcasted_iota(jnp.int32, (2 * Q_BLOCK, Q_BLOCK), 0) % Q_BLOCK
    s_idx = lax.broadcasted_iota(jnp.int32, (2 * Q_BLOCK, Q_BLOCK), 1)
    causal = s_idx < t_idx

    def stack_heads(x):
        zero = jnp.zeros_like(x)
        keep = lax.broadcasted_iota(jnp.int32, x.shape, 1) < HEAD_DIM
        return jnp.concatenate([jnp.where(keep, x, zero), jnp.where(keep, zero, x)], axis=0)

    def load_kv(j):
        start = j * Q_BLOCK
        if not isinstance(start, int):
            start = pl.multiple_of(start, Q_BLOCK)
        rows = pl.ds(start, Q_BLOCK)
        return k_ref[rows, :], stack_heads(v_ref[rows, :])

    def score(q2, kj):
        return lax.dot_general(q2, kj, (((1,), (1,)), ((), ())), preferred_element_type=_F32)

    def log_terms(z, mask):
        y = jnp.maximum(z, 0.0) + jnp.log(1.0 + jnp.exp2(jnp.abs(z) * (-LOG2_E)))
        return (y if mask is None else jnp.where(mask, y, 0.0)), z - y

    def later_sums(y):
        return _dot(y.astype(_BF16), suffix_mat)

    def weights(log_sig, later, dead, mask):
        drop = later if dead is None else later + dead
        a = jnp.exp(log_sig - drop)
        if mask is not None:
            a = jnp.where(mask, a, 0.0)
        a = a.astype(_BF16)
        rows = a.shape[0] // HEADS_PER_STEP
        return jnp.concatenate([a[:rows], a[rows:]], axis=1)

    def first_rows(x, rows):
        return x if rows == Q_BLOCK else jnp.concatenate(
            [x[:rows], x[Q_BLOCK:Q_BLOCK + rows]], axis=0)

    def add_first_rows(x, part, rows):
        if rows == Q_BLOCK:
            return x + part
        pieces = []
        for h in range(x.shape[0] // Q_BLOCK):
            pieces += [x[h * Q_BLOCK:h * Q_BLOCK + rows] + part[h * rows:(h + 1) * rows],
                       x[h * Q_BLOCK + rows:(h + 1) * Q_BLOCK]]
        return jnp.concatenate(pieces, axis=0)

    def visit_rows(back):
        return ATTN_LAST_VISIT_ROWS if back == ATTN_UNROLLED_VISITS - 1 else Q_BLOCK

    def unrolled_visits(first_block):
        visits = [(g, back) for g in range(ATTN_Q_BLOCKS) for back in range(ATTN_UNROLLED_VISITS)
                  if not (isinstance(first_block, int) and first_block + g - back < 0)]
        kv, z, log_sig, later, total, acc, dead = {}, {}, {}, {}, {}, {}, {}

        def key_block(v):
            e = v[0] - v[1]
            if e not in kv:
                kv[e] = load_kv(first_block + e)
            return kv[e]

        def mask_of(v):
            return causal if v[1] == 0 else None

        def scores(v):
            g, back = v
            q = q_ref[g * Q_BLOCK:g * Q_BLOCK + visit_rows(back), :]
            z[v] = score(stack_heads(q), key_block(v)[0])

        def suffix_sums(v):
            y, log_sig[v] = log_terms(z.pop(v), mask_of(v))
            total[v] = jnp.sum(y, axis=-1, keepdims=True)
            later[v] = later_sums(y)

        def outputs(v):
            g, back = v
            rows = visit_rows(back)
            before = None if g not in dead else first_rows(dead[g], rows)
            a = weights(log_sig.pop(v), later.pop(v), before, mask_of(v))
            part = _dot(a, key_block(v)[1])
            acc[g] = part if g not in acc else add_first_rows(acc[g], part, rows)
            dead[g] = total.pop(v) if g not in dead else add_first_rows(dead[g], total.pop(v), rows)

        stages = (scores, suffix_sums, outputs)
        for slot in range(len(visits) + ATTN_STAGE_LAG * (len(stages) - 1)):
            for n, stage in enumerate(stages):
                if 0 <= slot - n * ATTN_STAGE_LAG < len(visits):
                    stage(visits[slot - n * ATTN_STAGE_LAG])

        most_alive = None
        for g in range(ATTN_Q_BLOCKS):
            acc_ref[g] = acc[g]
            dead_ref[g] = dead[g]
            g_min = jnp.min(dead[g])
            most_alive = g_min if most_alive is None else jnp.minimum(most_alive, g_min)
        return most_alive < DEAD_LOG

    def visit(q2, kv, dead, mask):
        y, log_sig = log_terms(score(q2, kv[0]), mask)
        a = weights(log_sig, later_sums(y), dead, mask)
        return _dot(a, kv[1]), dead + jnp.sum(y, axis=-1, keepdims=True)

    def remaining_visits(first_block):
        for g in range(ATTN_Q_BLOCKS):
            last_block = first_block + g - (ATTN_UNROLLED_VISITS - 1)
            if isinstance(last_block, int) and last_block < 0:
                continue
            q2 = stack_heads(q_ref[g * Q_BLOCK:(g + 1) * Q_BLOCK, :])

            def step(j, mask, g=g, q2=q2):
                contrib, dead = visit(q2, load_kv(j), dead_ref[g], mask)
                acc_ref[g] += contrib
                dead_ref[g] = dead
                return jnp.min(dead) < DEAD_LOG

            def cond(carry):
                j, alive = carry
                return jnp.logical_and(j >= 0, alive)

            def body(carry, step=step):
                j, _ = carry
                return j - 1, step(j, None)

            @pl.when(jnp.min(dead_ref[g]) < DEAD_LOG)
            def _(last_block=last_block, step=step, cond=cond, body=body):
                alive = step(last_block, t_idx >= ATTN_LAST_VISIT_ROWS)
                lax.while_loop(cond, body, (last_block - 1, alive))

    @pl.when(ig == 0)
    def _():
        pl.when(unrolled_visits(0))(lambda: remaining_visits(0))

    @pl.when(ig > 0)
    def _():
        first_block = ig * ATTN_Q_BLOCKS
        pl.when(unrolled_visits(first_block))(lambda: remaining_visits(first_block))

    for g in range(ATTN_Q_BLOCKS):
        o_ref[g * Q_BLOCK:(g + 1) * Q_BLOCK, :] = acc_ref[g].astype(o_ref.dtype)


def _attention(q, k, v):
    blocks, b, s, _ = q.shape
    group_rows = ATTN_Q_BLOCKS * Q_BLOCK
    assert ATTN_Q_BLOCKS >= ATTN_UNROLLED_VISITS - 1
    assert s % group_rows == 0 and q.shape[-1] == LANES
    assert 0 < ATTN_LAST_VISIT_ROWS <= Q_BLOCK and ATTN_LAST_VISIT_ROWS % BF16_SUBLANES == 0
    q_spec = pl.BlockSpec((None, None, group_rows, LANES), lambda bi, c, i: (c, bi, i, 0))
    kv_spec = pl.BlockSpec((None, None, s, LANES), lambda bi, c, i: (c, bi, 0, 0))
    return pl.pallas_call(
        _attn_kernel,
        grid=(b, blocks, s // group_rows),
        in_specs=[q_spec, kv_spec, kv_spec],
        out_specs=q_spec,
        out_shape=jax.ShapeDtypeStruct(q.shape, _BF16),
        scratch_shapes=[pltpu.VMEM((ATTN_Q_BLOCKS, Q_BLOCK, LANES), _F32),
                        pltpu.VMEM((ATTN_Q_BLOCKS, HEADS_PER_STEP * Q_BLOCK, 1), _F32)],
        compiler_params=pltpu.CompilerParams(
            dimension_semantics=("arbitrary", "arbitrary", "arbitrary"),
            vmem_limit_bytes=VMEM_LIMIT_BYTES),
        name="attn",
    )(q, k, v)


def kernel(x, ffn1_norm, ffn1_w_in, ffn1_w_out, ffn2_norm, ffn2_w_in, ffn2_w_out, pool_norm,
           pool_w, pool_scale, kv_norm, w_kv, k_gain, attn_norm, w_q, q_gain, w_o):
    b, s, d = x.shape
    assert d == N_HEADS * HEAD_DIM
    h = x.reshape(b * s, d)

    h = _ffn(h, ffn1_norm[0], ffn1_w_in[0], ffn1_w_out[0])
    h = _pool_ffn(h, s, pool_norm[0], pool_w[0], pool_scale[0],
                  ffn2_norm[0], ffn2_w_in[0], ffn2_w_out[0])

    k, v = _head_proj(h, kv_norm, w_kv, k_gain, d_norm=d)

    h = _ffn(h, ffn1_norm[1], ffn1_w_in[1], ffn1_w_out[1])
    (q,) = _head_proj(h, attn_norm[0], w_q[0], q_gain[0] * (HEAD_DIM ** -0.5), d_norm=d)
    per_batch = lambda a: a.reshape(a.shape[0], b, s, LANES)
    o = _attention(per_batch(q), per_batch(k), per_batch(v))
    h = _proj_ffn(h, o.reshape(o.shape[0], b * s, LANES), w_o[0],
                  ffn2_norm[1], ffn2_w_in[1], ffn2_w_out[1])
    return h.reshape(b, s, d)
```

```python
import functools

import jax
import jax.numpy as jnp
from jax import lax
from jax.experimental import pallas as pl
from jax.experimental.pallas import tpu as pltpu

N_HEADS = 16
HEAD_DIM = 64
POOL_WINDOWS = (2, 4, 8, 16)
MAX_WINDOW = max(POOL_WINDOWS)
Q_BLOCK = 128
RMS_EPS = 1e-6
FFN_RES_WEIGHT = 0.5

LANES = 128
BF16_SUBLANES = 16
MXU_DIM = 256
VMEM_LIMIT_BYTES = 56 * 1024 * 1024

FFN_ROWS = 1024
FFN_CHUNK = MXU_DIM
POOL_ROW_BLOCKS = 2
PROJ_ROWS = 1024
HEADS_PER_STEP = LANES // HEAD_DIM
ATTN_Q_BLOCKS = 16
ATTN_UNROLLED_VISITS = 3
ATTN_LAST_VISIT_ROWS = 32

DEAD_LOG = 105.0
LOG2_E = 1.4426950408889634

_BF16 = jnp.bfloat16
_F32 = jnp.float32


def _resident(shape):
    zeros = (0,) * len(shape)
    return pl.BlockSpec(shape, lambda *_: zeros, pipeline_mode=pl.Buffered(1))


def _rms_norm(x, gain):
    ms = jnp.mean(x * x, axis=-1, keepdims=True)
    return x * lax.rsqrt(ms + RMS_EPS) * gain


def _dot(a, b):
    return jnp.dot(a, b, preferred_element_type=_F32)


def _ffn_update(x, g_ref, win_ref, wout_ref, o_ref, side_work=()):
    d_ff = wout_ref.shape[0]
    n_chunks = d_ff // FFN_CHUNK
    emit_after = {(k + 1) * n_chunks // (len(side_work) + 1) - 1: work
                  for k, work in enumerate(side_work)}
    xn = _rms_norm(x, g_ref[...]).astype(_BF16)
    acc = x * (1.0 / FFN_RES_WEIGHT)
    for c in range(n_chunks):
        lo = c * FFN_CHUNK
        gate = _dot(xn, win_ref[:, lo:lo + FFN_CHUNK])
        up = _dot(xn, win_ref[:, d_ff + lo:d_ff + lo + FFN_CHUNK])
        act = (gate * jax.nn.sigmoid(gate) * up).astype(_BF16)
        acc = acc + _dot(act, wout_ref[lo:lo + FFN_CHUNK, :])
        if c in emit_after:
            emit_after[c]()
    o_ref[...] = FFN_RES_WEIGHT * acc


def _pool_pieces(x_ref, halo_ref, first_of_seq, g_ref, w_ref, scale_ref, o_ref):
    group_w = w_ref.shape[1]
    block = x_ref.shape[0] // POOL_ROW_BLOCKS
    shared = {}

    def normed(rb):
        if rb not in shared:
            gain = g_ref[...]
            if rb == 0:
                before = jnp.where(first_of_seq, 0.0, _rms_norm(halo_ref[...], gain))
                shared[rb] = jnp.concatenate([before, _rms_norm(x_ref[:block], gain)], axis=0)
            else:
                shared[rb] = _rms_norm(x_ref[rb * block - MAX_WINDOW:(rb + 1) * block], gain)
        return shared[rb]

    def piece(rb, g, w):
        ext = normed(rb)
        cols = slice(g * group_w, (g + 1) * group_w)
        rows = slice(rb * block, (rb + 1) * block)
        s = ext[:, cols]
        shift = 1
        while shift < w:
            s = s + pltpu.roll(s, shift, axis=0)
            shift *= 2
        if rb == 0:
            row = lax.broadcasted_iota(jnp.int32, (block, 1), 0)
            cnt = jnp.where(first_of_seq, jnp.minimum(row + 1, w), w)
            mean = s[MAX_WINDOW:] * (1.0 / cnt.astype(_F32))
        else:
            mean = s[MAX_WINDOW:] * (1.0 / w)
        y = _dot((mean - ext[MAX_WINDOW:, cols]).astype(_BF16), w_ref[g])
        o_ref[rows, cols] = x_ref[rows, cols] + y * scale_ref[:, cols]

    return [functools.partial(piece, rb, g, w)
            for rb in range(POOL_ROW_BLOCKS) for g, w in enumerate(POOL_WINDOWS)]


def _ffn_kernel(x_ref, g_ref, win_ref, wout_ref, o_ref):
    _ffn_update(x_ref[...], g_ref, win_ref, wout_ref, o_ref)


def _proj_ffn_kernel(x_ref, a_ref, wp_ref, g_ref, win_ref, wout_ref, o_ref):
    a = jnp.concatenate([a_ref[c] for c in range(a_ref.shape[0])], axis=1)
    x = x_ref[...] + _dot(a, wp_ref[...])
    _ffn_update(x, g_ref, win_ref, wout_ref, o_ref)


def _pool_ffn_kernel(x_ref, halo_ref, pg_ref, pw_ref, ps_ref, g_ref, win_ref, wout_ref, o_ref,
                     pooled_ref, *, tiles_per_seq):
    i = pl.program_id(0)
    tile = jnp.minimum(i, pl.num_programs(0) - 2)
    first_of_seq = tile % tiles_per_seq == 0

    def pool_pieces():
        return _pool_pieces(x_ref, halo_ref, first_of_seq, pg_ref, pw_ref, ps_ref, pooled_ref)

    @pl.when(i == 0)
    def _():
        for piece in pool_pieces():
            piece()

    @pl.when(i > 0)
    def _():
        _ffn_update(pooled_ref[...], g_ref, win_ref, wout_ref, o_ref, side_work=pool_pieces())


def _layer_resident(stacked, layer):
    index = (layer,) + (0,) * (stacked.ndim - 1)
    return pl.BlockSpec((None,) + stacked.shape[1:], lambda *_: index,
                        pipeline_mode=pl.Buffered(1))


def _ffn_call(body, name, t, d, steps, row_maps, specs, args, gain, w_in, w_out, layer,
              scratch=()):
    d_ff = w_out.shape[1]
    assert t % FFN_ROWS == 0 and d_ff % FFN_CHUNK == 0 and w_in.shape[1:] == (d, 2 * d_ff)
    assert w_in.dtype == _BF16 and w_out.dtype == _BF16
    ffn_specs = [_resident((1, d)), _layer_resident(w_in, layer), _layer_resident(w_out, layer)]
    return pl.pallas_call(
        body,
        grid=(steps,),
        in_specs=specs + ffn_specs,
        out_specs=pl.BlockSpec((FFN_ROWS, d), row_maps),
        out_shape=jax.ShapeDtypeStruct((t, d), _F32),
        scratch_shapes=list(scratch),
        compiler_params=pltpu.CompilerParams(
            dimension_semantics=("arbitrary",), vmem_limit_bytes=VMEM_LIMIT_BYTES),
        name=name,
    )(*args, gain.reshape(1, d), w_in, w_out)


def _ffn(h, gain, w_in, w_out, layer):
    t, d = h.shape
    rows = lambda i: (i, 0)
    return _ffn_call(_ffn_kernel, "ffn", t, d, t // FFN_ROWS, rows,
                     [pl.BlockSpec((FFN_ROWS, d), rows)], (h,), gain, w_in, w_out, layer)


def _proj_ffn(h, mixer_out, w_proj, gain, w_in, w_out, layer):
    t, d = h.shape
    blocks = mixer_out.shape[0]
    assert mixer_out.shape == (blocks, t, LANES) and w_proj.shape == (blocks * LANES, d)
    rows = lambda i: (i, 0)
    specs = [pl.BlockSpec((FFN_ROWS, d), rows),
             pl.BlockSpec((blocks, FFN_ROWS, LANES), lambda i: (0, i, 0)),
             _resident(w_proj.shape)]
    return _ffn_call(_proj_ffn_kernel, "proj_ffn", t, d, t // FFN_ROWS, rows, specs,
                     (h, mixer_out, w_proj.astype(_BF16)), gain, w_in, w_out, layer)


def _pool_ffn(h, seq_len, pool_gain, pool_w, pool_scale, gain, w_in, w_out, layer):
    t, d = h.shape
    assert seq_len % FFN_ROWS == 0 and FFN_ROWS % MAX_WINDOW == 0 and t % seq_len == 0
    tiles = t // FFN_ROWS
    halo_blocks = FFN_ROWS // MAX_WINDOW
    pooled_tile = lambda i: jnp.minimum(i, tiles - 1)
    specs = [pl.BlockSpec((FFN_ROWS, d), lambda i: (pooled_tile(i), 0)),
             pl.BlockSpec((MAX_WINDOW, d),
                          lambda i: (jnp.maximum(pooled_tile(i) * halo_blocks - 1, 0), 0)),
             _resident((1, d)), _resident(pool_w.shape), _resident((1, d))]
    args = (h, h, pool_gain.reshape(1, d), pool_w.astype(_BF16), pool_scale.reshape(1, d))
    return _ffn_call(
        functools.partial(_pool_ffn_kernel, tiles_per_seq=seq_len // FFN_ROWS), "pool_ffn",
        t, d, tiles + 1, lambda i: (jnp.maximum(i - 1, 0), 0), specs, args, gain, w_in, w_out,
        layer, scratch=[pltpu.VMEM((FFN_ROWS, d), _F32)])


def _head_proj_kernel(x_ref, g_ref, w_ref, hg_ref, *o_refs, d_norm):
    xn = _rms_norm(x_ref[...], g_ref[...]).astype(_BF16)
    y = _dot(xn, w_ref[...])
    first_head = lax.broadcasted_iota(jnp.int32, (y.shape[0], LANES), 1) < HEAD_DIM
    for c in range(d_norm // LANES):
        cols = slice(c * LANES, (c + 1) * LANES)
        yc = y[:, cols]
        sq = yc * yc
        ms = [jnp.sum(jnp.where(keep, sq, 0.0), axis=-1, keepdims=True) * (1.0 / HEAD_DIM)
              for keep in (first_head, ~first_head)]
        r = jnp.where(first_head, lax.rsqrt(ms[0] + RMS_EPS), lax.rsqrt(ms[1] + RMS_EPS))
        o_refs[0][c] = (yc * r * hg_ref[:, cols]).astype(o_refs[0].dtype)
    if len(o_refs) > 1:
        for c in range(o_refs[1].shape[0]):
            o_refs[1][c] = y[:, d_norm + c * LANES:d_norm + (c + 1) * LANES].astype(
                o_refs[1].dtype)


def _head_proj(h, gain, w, head_gain, d_norm):
    t, d = h.shape
    d_out = w.shape[1]
    assert t % PROJ_ROWS == 0 and HEADS_PER_STEP == 2
    assert d_norm % LANES == 0 and (d_out - d_norm) % LANES == 0
    hg = jnp.tile(head_gain, d_norm // HEAD_DIM).reshape(1, d_norm)
    row_spec = pl.BlockSpec((PROJ_ROWS, d), lambda i: (i, 0))
    widths = [d_norm] + ([d_out - d_norm] if d_out > d_norm else [])
    out_shapes = [jax.ShapeDtypeStruct((n // LANES, t, LANES), _BF16) for n in widths]
    out_specs = [pl.BlockSpec((n // LANES, PROJ_ROWS, LANES), lambda i: (0, i, 0)) for n in widths]
    return pl.pallas_call(
        functools.partial(_head_proj_kernel, d_norm=d_norm),
        grid=(t // PROJ_ROWS,),
        in_specs=[row_spec, _resident((1, d)), _resident(w.shape), _resident((1, d_norm))],
        out_specs=out_specs,
        out_shape=out_shapes,
        compiler_params=pltpu.CompilerParams(
            dimension_semantics=("arbitrary",), vmem_limit_bytes=VMEM_LIMIT_BYTES),
        name="head_proj",
    )(h, gain.reshape(1, d), w.astype(_BF16), hg)


def _attn_kernel(q_ref, k_ref, v_ref, o_ref, acc_ref, dead_ref):
    ig = pl.program_id(2)

    src = lax.broadcasted_iota(jnp.int32, (Q_BLOCK, Q_BLOCK), 0)
    dst = lax.broadcasted_iota(jnp.int32, (Q_BLOCK, Q_BLOCK), 1)
    suffix_mat = (src > dst).astype(_BF16)

    t_idx = lax.broadcasted_iota(jnp.int32, (2 * Q_BLOCK, Q_BLOCK), 0) % Q_BLOCK
    s_idx = lax.broadcasted_iota(jnp.int32, (2 * Q_BLOCK, Q_BLOCK), 1)
    causal = s_idx < t_idx

    def stack_heads(x):
        zero = jnp.zeros_like(x)
        keep = lax.broadcasted_iota(jnp.int32, x.shape, 1) < HEAD_DIM
        return jnp.concatenate([jnp.where(keep, x, zero), jnp.where(keep, zero, x)], axis=0)

    def load_kv(j):
        start = j * Q_BLOCK
        if not isinstance(start, int):
            start = pl.multiple_of(start, Q_BLOCK)
        rows = pl.ds(start, Q_BLOCK)
        return k_ref[rows, :], stack_heads(v_ref[rows, :])

    def score(q2, kj):
        return lax.dot_general(q2, kj, (((1,), (1,)), ((), ())), preferred_element_type=_F32)

    def log_terms(z, mask):
        y = jnp.maximum(z, 0.0) + jnp.log(1.0 + jnp.exp2(jnp.abs(z) * (-LOG2_E)))
        return (y if mask is None else jnp.where(mask, y, 0.0)), z - y

    def later_sums(y):
        return _dot(y.astype(_BF16), suffix_mat)

    def weights(log_sig, later, dead, mask):
        drop = later if dead is None else later + dead
        a = jnp.exp(log_sig - drop)
        if mask is not None:
            a = jnp.where(mask, a, 0.0)
        a = a.astype(_BF16)
        rows = a.shape[0] // HEADS_PER_STEP
        return jnp.concatenate([a[:rows], a[rows:]], axis=1)

    def first_rows(x, rows):
        return x if rows == Q_BLOCK else jnp.concatenate(
            [x[:rows], x[Q_BLOCK:Q_BLOCK + rows]], axis=0)

    def add_first_rows(x, part, rows):
        if rows == Q_BLOCK:
            return x + part
        pieces = []
        for h in range(x.shape[0] // Q_BLOCK):
            pieces += [x[h * Q_BLOCK:h * Q_BLOCK + rows] + part[h * rows:(h + 1) * rows],
                       x[h * Q_BLOCK + rows:(h + 1) * Q_BLOCK]]
        return jnp.concatenate(pieces, axis=0)

    def visit_rows(back):
        return ATTN_LAST_VISIT_ROWS if back == ATTN_UNROLLED_VISITS - 1 else Q_BLOCK

    def rows_of(parts):
        return parts[0] if len(parts) == 1 else jnp.concatenate(parts, axis=0)

    def split_rows(x, visits, rows_per_query):
        out, at = {}, 0
        for v in visits:
            n = visit_rows(v[1]) * rows_per_query
            out[v] = x[at:at + n]
            at += n
        return out

    def unrolled_visits(first_block):
        users = {}
        for g in range(ATTN_Q_BLOCKS):
            for back in range(ATTN_UNROLLED_VISITS):
                if isinstance(first_block, int) and first_block + g - back < 0:
                    continue
                users.setdefault(g - back, []).append((g, back))
        offsets = sorted(users, reverse=True)
        order = [v for e in offsets for v in users[e]]
        q = [q_ref[g * Q_BLOCK:(g + 1) * Q_BLOCK, :] for g in range(ATTN_Q_BLOCKS)]
        q2 = {v: stack_heads(q[v[0]][:visit_rows(v[1])]) for v in order}
        kv = {e: load_kv(first_block + e) for e in offsets}

        z = {}
        for e in offsets:
            zc = score(rows_of([q2[v] for v in users[e]]), kv[e][0])
            z.update(split_rows(zc, users[e], HEADS_PER_STEP))
        y, log_sig = {}, {}
        for v in order:
            y[v], log_sig[v] = log_terms(z[v], causal if v[1] == 0 else None)
        later = split_rows(later_sums(rows_of([y[v] for v in order])), order, HEADS_PER_STEP)
        total = {v: jnp.sum(y[v], axis=-1, keepdims=True) for v in order}

        a_wide, dead_end = {}, {}
        for g in range(ATTN_Q_BLOCKS):
            dead = None
            for back in range(ATTN_UNROLLED_VISITS):
                v = (g, back)
                if v in z:
                    rows = visit_rows(back)
                    a_wide[v] = weights(log_sig[v], later[v],
                                        None if dead is None else first_rows(dead, rows),
                                        causal if back == 0 else None)
                    dead = total[v] if dead is None else add_first_rows(dead, total[v], rows)
            dead_end[g] = dead

        acc = {}
        for e in offsets:
            c = split_rows(_dot(rows_of([a_wide[v] for v in users[e]]), kv[e][1]), users[e], 1)
            for v in users[e]:
                g = v[0]
                acc[g] = c[v] if g not in acc else add_first_rows(acc[g], c[v], visit_rows(v[1]))

        most_alive = None
        for g in range(ATTN_Q_BLOCKS):
            acc_ref[g] = acc[g]
            dead_ref[g] = dead_end[g]
            g_min = jnp.min(dead_end[g])
            most_alive = g_min if most_alive is None else jnp.minimum(most_alive, g_min)
        return most_alive < DEAD_LOG

    def visit(q2, kv, dead, mask):
        y, log_sig = log_terms(score(q2, kv[0]), mask)
        a = weights(log_sig, later_sums(y), dead, mask)
        return _dot(a, kv[1]), dead + jnp.sum(y, axis=-1, keepdims=True)

    def remaining_visits(first_block):
        for g in range(ATTN_Q_BLOCKS):
            last_block = first_block + g - (ATTN_UNROLLED_VISITS - 1)
            if isinstance(last_block, int) and last_block < 0:
                continue
            q2 = stack_heads(q_ref[g * Q_BLOCK:(g + 1) * Q_BLOCK, :])

            def step(j, mask, g=g, q2=q2):
                contrib, dead = visit(q2, load_kv(j), dead_ref[g], mask)
                acc_ref[g] += contrib
                dead_ref[g] = dead
                return jnp.min(dead) < DEAD_LOG

            def cond(carry):
                j, alive = carry
                return jnp.logical_and(j >= 0, alive)

            def body(carry, step=step):
                j, _ = carry
                return j - 1, step(j, None)

            @pl.when(jnp.min(dead_ref[g]) < DEAD_LOG)
            def _(last_block=last_block, step=step, cond=cond, body=body):
                alive = step(last_block, t_idx >= ATTN_LAST_VISIT_ROWS)
                lax.while_loop(cond, body, (last_block - 1, alive))

    @pl.when(ig == 0)
    def _():
        pl.when(unrolled_visits(0))(lambda: remaining_visits(0))

    @pl.when(ig > 0)
    def _():
        first_block = ig * ATTN_Q_BLOCKS
        pl.when(unrolled_visits(first_block))(lambda: remaining_visits(first_block))

    for g in range(ATTN_Q_BLOCKS):
        o_ref[g * Q_BLOCK:(g + 1) * Q_BLOCK, :] = acc_ref[g].astype(o_ref.dtype)


def _attention(q, k, v):
    blocks, b, s, _ = q.shape
    group_rows = ATTN_Q_BLOCKS * Q_BLOCK
    assert ATTN_Q_BLOCKS >= ATTN_UNROLLED_VISITS - 1
    assert s % group_rows == 0 and q.shape[-1] == LANES
    assert 0 < ATTN_LAST_VISIT_ROWS <= Q_BLOCK and ATTN_LAST_VISIT_ROWS % BF16_SUBLANES == 0
    q_spec = pl.BlockSpec((None, None, group_rows, LANES), lambda bi, c, i: (c, bi, i, 0))
    kv_spec = pl.BlockSpec((None, None, s, LANES), lambda bi, c, i: (c, bi, 0, 0))
    return pl.pallas_call(
        _attn_kernel,
        grid=(b, blocks, s // group_rows),
        in_specs=[q_spec, kv_spec, kv_spec],
        out_specs=q_spec,
        out_shape=jax.ShapeDtypeStruct(q.shape, _BF16),
        scratch_shapes=[pltpu.VMEM((ATTN_Q_BLOCKS, Q_BLOCK, LANES), _F32),
                        pltpu.VMEM((ATTN_Q_BLOCKS, HEADS_PER_STEP * Q_BLOCK, 1), _F32)],
        compiler_params=pltpu.CompilerParams(
            dimension_semantics=("arbitrary", "arbitrary", "arbitrary"),
            vmem_limit_bytes=VMEM_LIMIT_BYTES),
        name="attn",
    )(q, k, v)


def kernel(x, ffn1_norm, ffn1_w_in, ffn1_w_out, ffn2_norm, ffn2_w_in, ffn2_w_out, pool_norm,
           pool_w, pool_scale, kv_norm, w_kv, k_gain, attn_norm, w_q, q_gain, w_o):
    b, s, d = x.shape
    assert d == N_HEADS * HEAD_DIM
    h = x.reshape(b * s, d)
    w1_in, w1_out, w2_in, w2_out = (
        w.astype(_BF16) for w in (ffn1_w_in, ffn1_w_out, ffn2_w_in, ffn2_w_out))

    h = _ffn(h, ffn1_norm[0], w1_in, w1_out, layer=0)
    h = _pool_ffn(h, s, pool_norm[0], pool_w[0], pool_scale[0], ffn2_norm[0], w2_in, w2_out,
                  layer=0)

    k, v = _head_proj(h, kv_norm, w_kv, k_gain, d_norm=d)

    h = _ffn(h, ffn1_norm[1], w1_in, w1_out, layer=1)
    (q,) = _head_proj(h, attn_norm[0], w_q[0], q_gain[0] * (HEAD_DIM ** -0.5), d_norm=d)
    per_batch = lambda a: a.reshape(a.shape[0], b, s, LANES)
    o = _attention(per_batch(q), per_batch(k), per_batch(v))
    h = _proj_ffn(h, o.reshape(o.shape[0], b * s, LANES), w_o[0], ffn2_norm[1], w2_in, w2_out,
                  layer=1)
    return h.reshape(b, s, d)
```

```python
import functools

import jax
import jax.numpy as jnp
from jax import lax
from jax.experimental import pallas as pl
from jax.experimental.pallas import tpu as pltpu

N_HEADS = 16
HEAD_DIM = 64
POOL_WINDOWS = (2, 4, 8, 16)
MAX_WINDOW = max(POOL_WINDOWS)
Q_BLOCK = 128
RMS_EPS = 1e-6
FFN_RES_WEIGHT = 0.5

LANES = 128
BF16_SUBLANES = 16
MXU_DIM = 256
VMEM_LIMIT_BYTES = 56 * 1024 * 1024

FFN_ROWS = 1024
FFN_CHUNK = MXU_DIM
POOL_ROW_BLOCKS = 2
PROJ_ROWS = 1024
HEADS_PER_STEP = LANES // HEAD_DIM
ATTN_Q_BLOCKS = 32
ATTN_UNROLLED_VISITS = 3
ATTN_LAST_VISIT_ROWS = 32

DEAD_LOG = 105.0
LOG2_E = 1.4426950408889634

_BF16 = jnp.bfloat16
_F32 = jnp.float32


def _resident(shape):
    zeros = (0,) * len(shape)
    return pl.BlockSpec(shape, lambda *_: zeros, pipeline_mode=pl.Buffered(1))


def _rms_norm(x, gain):
    ms = jnp.mean(x * x, axis=-1, keepdims=True)
    return x * lax.rsqrt(ms + RMS_EPS) * gain


def _dot(a, b):
    return jnp.dot(a, b, preferred_element_type=_F32)


def _ffn_update(x, g_ref, win_ref, wout_ref, o_ref, side_work=()):
    d_ff = wout_ref.shape[0]
    n_chunks = d_ff // FFN_CHUNK
    emit_after = {(k + 1) * n_chunks // (len(side_work) + 1) - 1: work
                  for k, work in enumerate(side_work)}
    xn = _rms_norm(x, g_ref[...]).astype(_BF16)
    acc = x * (1.0 / FFN_RES_WEIGHT)
    for c in range(n_chunks):
        lo = c * FFN_CHUNK
        gate = _dot(xn, win_ref[:, lo:lo + FFN_CHUNK])
        up = _dot(xn, win_ref[:, d_ff + lo:d_ff + lo + FFN_CHUNK])
        act = (gate * jax.nn.sigmoid(gate) * up).astype(_BF16)
        acc = acc + _dot(act, wout_ref[lo:lo + FFN_CHUNK, :])
        if c in emit_after:
            emit_after[c]()
    o_ref[...] = FFN_RES_WEIGHT * acc


def _pool_pieces(x_ref, halo_ref, first_of_seq, g_ref, w_ref, scale_ref, o_ref):
    group_w = w_ref.shape[1]
    block = x_ref.shape[0] // POOL_ROW_BLOCKS
    shared = {}

    def normed(rb):
        if rb not in shared:
            gain = g_ref[...]
            if rb == 0:
                before = jnp.where(first_of_seq, 0.0, _rms_norm(halo_ref[...], gain))
                shared[rb] = jnp.concatenate([before, _rms_norm(x_ref[:block], gain)], axis=0)
            else:
                shared[rb] = _rms_norm(x_ref[rb * block - MAX_WINDOW:(rb + 1) * block], gain)
        return shared[rb]

    def piece(rb, g, w):
        ext = normed(rb)
        cols = slice(g * group_w, (g + 1) * group_w)
        rows = slice(rb * block, (rb + 1) * block)
        s = ext[:, cols]
        shift = 1
        while shift < w:
            s = s + pltpu.roll(s, shift, axis=0)
            shift *= 2
        if rb == 0:
            row = lax.broadcasted_iota(jnp.int32, (block, 1), 0)
            cnt = jnp.where(first_of_seq, jnp.minimum(row + 1, w), w)
            mean = s[MAX_WINDOW:] * (1.0 / cnt.astype(_F32))
        else:
            mean = s[MAX_WINDOW:] * (1.0 / w)
        y = _dot((mean - ext[MAX_WINDOW:, cols]).astype(_BF16), w_ref[g])
        o_ref[rows, cols] = x_ref[rows, cols] + y * scale_ref[:, cols]

    return [functools.partial(piece, rb, g, w)
            for rb in range(POOL_ROW_BLOCKS) for g, w in enumerate(POOL_WINDOWS)]


def _ffn_kernel(x_ref, g_ref, win_ref, wout_ref, o_ref):
    _ffn_update(x_ref[...], g_ref, win_ref, wout_ref, o_ref)


def _proj_ffn_kernel(x_ref, a_ref, wp_ref, g_ref, win_ref, wout_ref, o_ref):
    a = jnp.concatenate([a_ref[c] for c in range(a_ref.shape[0])], axis=1)
    x = x_ref[...] + _dot(a, wp_ref[...])
    _ffn_update(x, g_ref, win_ref, wout_ref, o_ref)


def _pool_ffn_kernel(x_ref, halo_ref, pg_ref, pw_ref, ps_ref, g_ref, win_ref, wout_ref, o_ref,
                     pooled_ref, *, tiles_per_seq):
    i = pl.program_id(0)
    tile = jnp.minimum(i, pl.num_programs(0) - 2)
    first_of_seq = tile % tiles_per_seq == 0

    def pool_pieces():
        return _pool_pieces(x_ref, halo_ref, first_of_seq, pg_ref, pw_ref, ps_ref, pooled_ref)

    @pl.when(i == 0)
    def _():
        for piece in pool_pieces():
            piece()

    @pl.when(i > 0)
    def _():
        _ffn_update(pooled_ref[...], g_ref, win_ref, wout_ref, o_ref, side_work=pool_pieces())


def _layer_resident(stacked, layer):
    index = (layer,) + (0,) * (stacked.ndim - 1)
    return pl.BlockSpec((None,) + stacked.shape[1:], lambda *_: index,
                        pipeline_mode=pl.Buffered(1))


def _ffn_call(body, name, t, d, steps, row_maps, specs, args, gain, w_in, w_out, layer,
              scratch=()):
    d_ff = w_out.shape[1]
    assert t % FFN_ROWS == 0 and d_ff % FFN_CHUNK == 0 and w_in.shape[1:] == (d, 2 * d_ff)
    assert w_in.dtype == _BF16 and w_out.dtype == _BF16
    ffn_specs = [_resident((1, d)), _layer_resident(w_in, layer), _layer_resident(w_out, layer)]
    return pl.pallas_call(
        body,
        grid=(steps,),
        in_specs=specs + ffn_specs,
        out_specs=pl.BlockSpec((FFN_ROWS, d), row_maps),
        out_shape=jax.ShapeDtypeStruct((t, d), _F32),
        scratch_shapes=list(scratch),
        compiler_params=pltpu.CompilerParams(
            dimension_semantics=("arbitrary",), vmem_limit_bytes=VMEM_LIMIT_BYTES),
        name=name,
    )(*args, gain.reshape(1, d), w_in, w_out)


def _ffn(h, gain, w_in, w_out, layer):
    t, d = h.shape
    rows = lambda i: (i, 0)
    return _ffn_call(_ffn_kernel, "ffn", t, d, t // FFN_ROWS, rows,
                     [pl.BlockSpec((FFN_ROWS, d), rows)], (h,), gain, w_in, w_out, layer)


def _proj_ffn(h, mixer_out, w_proj, gain, w_in, w_out, layer):
    t, d = h.shape
    blocks = mixer_out.shape[0]
    assert mixer_out.shape == (blocks, t, LANES) and w_proj.shape == (blocks * LANES, d)
    rows = lambda i: (i, 0)
    specs = [pl.BlockSpec((FFN_ROWS, d), rows),
             pl.BlockSpec((blocks, FFN_ROWS, LANES), lambda i: (0, i, 0)),
             _resident(w_proj.shape)]
    return _ffn_call(_proj_ffn_kernel, "proj_ffn", t, d, t // FFN_ROWS, rows, specs,
                     (h, mixer_out, w_proj.astype(_BF16)), gain, w_in, w_out, layer)


def _pool_ffn(h, seq_len, pool_gain, pool_w, pool_scale, gain, w_in, w_out, layer):
    t, d = h.shape
    assert seq_len % FFN_ROWS == 0 and FFN_ROWS % MAX_WINDOW == 0 and t % seq_len == 0
    tiles = t // FFN_ROWS
    halo_blocks = FFN_ROWS // MAX_WINDOW
    pooled_tile = lambda i: jnp.minimum(i, tiles - 1)
    specs = [pl.BlockSpec((FFN_ROWS, d), lambda i: (pooled_tile(i), 0)),
             pl.BlockSpec((MAX_WINDOW, d),
                          lambda i: (jnp.maximum(pooled_tile(i) * halo_blocks - 1, 0), 0)),
             _resident((1, d)), _resident(pool_w.shape), _resident((1, d))]
    args = (h, h, pool_gain.reshape(1, d), pool_w.astype(_BF16), pool_scale.reshape(1, d))
    return _ffn_call(
        functools.partial(_pool_ffn_kernel, tiles_per_seq=seq_len // FFN_ROWS), "pool_ffn",
        t, d, tiles + 1, lambda i: (jnp.maximum(i - 1, 0), 0), specs, args, gain, w_in, w_out,
        layer, scratch=[pltpu.VMEM((FFN_ROWS, d), _F32)])


def _head_proj_kernel(x_ref, g_ref, w_ref, hg_ref, *o_refs, d_norm):
    xn = _rms_norm(x_ref[...], g_ref[...]).astype(_BF16)
    y = _dot(xn, w_ref[...])
    first_head = lax.broadcasted_iota(jnp.int32, (y.shape[0], LANES), 1) < HEAD_DIM
    for c in range(d_norm // LANES):
        cols = slice(c * LANES, (c + 1) * LANES)
        yc = y[:, cols]
        sq = yc * yc
        ms = [jnp.sum(jnp.where(keep, sq, 0.0), axis=-1, keepdims=True) * (1.0 / HEAD_DIM)
              for keep in (first_head, ~first_head)]
        r = jnp.where(first_head, lax.rsqrt(ms[0] + RMS_EPS), lax.rsqrt(ms[1] + RMS_EPS))
        o_refs[0][c] = (yc * r * hg_ref[:, cols]).astype(o_refs[0].dtype)
    if len(o_refs) > 1:
        for c in range(o_refs[1].shape[0]):
            o_refs[1][c] = y[:, d_norm + c * LANES:d_norm + (c + 1) * LANES].astype(
                o_refs[1].dtype)


def _head_proj(h, gain, w, head_gain, d_norm):
    t, d = h.shape
    d_out = w.shape[1]
    assert t % PROJ_ROWS == 0 and HEADS_PER_STEP == 2
    assert d_norm % LANES == 0 and (d_out - d_norm) % LANES == 0
    hg = jnp.tile(head_gain, d_norm // HEAD_DIM).reshape(1, d_norm)
    row_spec = pl.BlockSpec((PROJ_ROWS, d), lambda i: (i, 0))
    widths = [d_norm] + ([d_out - d_norm] if d_out > d_norm else [])
    out_shapes = [jax.ShapeDtypeStruct((n // LANES, t, LANES), _BF16) for n in widths]
    out_specs = [pl.BlockSpec((n // LANES, PROJ_ROWS, LANES), lambda i: (0, i, 0)) for n in widths]
    return pl.pallas_call(
        functools.partial(_head_proj_kernel, d_norm=d_norm),
        grid=(t // PROJ_ROWS,),
        in_specs=[row_spec, _resident((1, d)), _resident(w.shape), _resident((1, d_norm))],
        out_specs=out_specs,
        out_shape=out_shapes,
        compiler_params=pltpu.CompilerParams(
            dimension_semantics=("arbitrary",), vmem_limit_bytes=VMEM_LIMIT_BYTES),
        name="head_proj",
    )(h, gain.reshape(1, d), w.astype(_BF16), hg)


def _attn_kernel(q_ref, k_ref, v_ref, o_ref, acc_ref, dead_ref):
    ig = pl.program_id(2)

    src = lax.broadcasted_iota(jnp.int32, (Q_BLOCK, Q_BLOCK), 0)
    dst = lax.broadcasted_iota(jnp.int32, (Q_BLOCK, Q_BLOCK), 1)
    suffix_mat = (src > dst).astype(_BF16)

    t_idx = lax.broadcasted_iota(jnp.int32, (2 * Q_BLOCK, Q_BLOCK), 0) % Q_BLOCK
    s_idx = lax.broadcasted_iota(jnp.int32, (2 * Q_BLOCK, Q_BLOCK), 1)
    causal = s_idx < t_idx

    def stack_heads(x):
        zero = jnp.zeros_like(x)
        keep = lax.broadcasted_iota(jnp.int32, x.shape, 1) < HEAD_DIM
        return jnp.concatenate([jnp.where(keep, x, zero), jnp.where(keep, zero, x)], axis=0)

    def load_kv(j):
        start = j * Q_BLOCK
        if not isinstance(start, int):
            start = pl.multiple_of(start, Q_BLOCK)
        rows = pl.ds(start, Q_BLOCK)
        return k_ref[rows, :], stack_heads(v_ref[rows, :])

    def score(q2, kj):
        return lax.dot_general(q2, kj, (((1,), (1,)), ((), ())), preferred_element_type=_F32)

    def log_terms(z, mask):
        y = jnp.maximum(z, 0.0) + jnp.log(1.0 + jnp.exp2(jnp.abs(z) * (-LOG2_E)))
        return (y if mask is None else jnp.where(mask, y, 0.0)), z - y

    def later_sums(y):
        return _dot(y.astype(_BF16), suffix_mat)

    def weights(log_sig, later, dead, mask):
        drop = later if dead is None else later + dead
        a = jnp.exp(log_sig - drop)
        if mask is not None:
            a = jnp.where(mask, a, 0.0)
        a = a.astype(_BF16)
        rows = a.shape[0] // HEADS_PER_STEP
        return jnp.concatenate([a[:rows], a[rows:]], axis=1)

    def first_rows(x, rows):
        return x if rows == Q_BLOCK else jnp.concatenate(
            [x[:rows], x[Q_BLOCK:Q_BLOCK + rows]], axis=0)

    def add_first_rows(x, part, rows):
        if rows == Q_BLOCK:
            return x + part
        pieces = []
        for h in range(x.shape[0] // Q_BLOCK):
            pieces += [x[h * Q_BLOCK:h * Q_BLOCK + rows] + part[h * rows:(h + 1) * rows],
                       x[h * Q_BLOCK + rows:(h + 1) * Q_BLOCK]]
        return jnp.concatenate(pieces, axis=0)

    def visit_rows(back):
        return ATTN_LAST_VISIT_ROWS if back == ATTN_UNROLLED_VISITS - 1 else Q_BLOCK

    def rows_of(parts):
        return parts[0] if len(parts) == 1 else jnp.concatenate(parts, axis=0)

    def split_rows(x, visits, rows_per_query):
        out, at = {}, 0
        for v in visits:
            n = visit_rows(v[1]) * rows_per_query
            out[v] = x[at:at + n]
            at += n
        return out

    def unrolled_visits(first_block):
        users = {}
        for g in range(ATTN_Q_BLOCKS):
            for back in range(ATTN_UNROLLED_VISITS):
                if isinstance(first_block, int) and first_block + g - back < 0:
                    continue
                users.setdefault(g - back, []).append((g, back))
        offsets = sorted(users, reverse=True)
        order = [v for e in offsets for v in users[e]]
        q = [q_ref[g * Q_BLOCK:(g + 1) * Q_BLOCK, :] for g in range(ATTN_Q_BLOCKS)]
        q2 = {v: stack_heads(q[v[0]][:visit_rows(v[1])]) for v in order}
        kv = {e: load_kv(first_block + e) for e in offsets}

        z = {}
        for e in offsets:
            zc = score(rows_of([q2[v] for v in users[e]]), kv[e][0])
            z.update(split_rows(zc, users[e], HEADS_PER_STEP))
        y, log_sig = {}, {}
        for v in order:
            y[v], log_sig[v] = log_terms(z[v], causal if v[1] == 0 else None)
        later = split_rows(later_sums(rows_of([y[v] for v in order])), order, HEADS_PER_STEP)
        total = {v: jnp.sum(y[v], axis=-1, keepdims=True) for v in order}

        a_wide, dead_end = {}, {}
        for g in range(ATTN_Q_BLOCKS):
            dead = None
            for back in range(ATTN_UNROLLED_VISITS):
                v = (g, back)
                if v in z:
                    rows = visit_rows(back)
                    a_wide[v] = weights(log_sig[v], later[v],
                                        None if dead is None else first_rows(dead, rows),
                                        causal if back == 0 else None)
                    dead = total[v] if dead is None else add_first_rows(dead, total[v], rows)
            dead_end[g] = dead

        acc = {}
        for e in offsets:
            c = split_rows(_dot(rows_of([a_wide[v] for v in users[e]]), kv[e][1]), users[e], 1)
            for v in users[e]:
                g = v[0]
                acc[g] = c[v] if g not in acc else add_first_rows(acc[g], c[v], visit_rows(v[1]))

        most_alive = None
        for g in range(ATTN_Q_BLOCKS):
            acc_ref[g] = acc[g]
            dead_ref[g] = dead_end[g]
            g_min = jnp.min(dead_end[g])
            most_alive = g_min if most_alive is None else jnp.minimum(most_alive, g_min)
        return most_alive < DEAD_LOG

    def visit(q2, kv, dead, mask):
        y, log_sig = log_terms(score(q2, kv[0]), mask)
        a = weights(log_sig, later_sums(y), dead, mask)
        return _dot(a, kv[1]), dead + jnp.sum(y, axis=-1, keepdims=True)

    def remaining_visits(first_block):
        for g in range(ATTN_Q_BLOCKS):
            last_block = first_block + g - (ATTN_UNROLLED_VISITS - 1)
            if isinstance(last_block, int) and last_block < 0:
                continue
            q2 = stack_heads(q_ref[g * Q_BLOCK:(g + 1) * Q_BLOCK, :])

            def step(j, mask, g=g, q2=q2):
                contrib, dead = visit(q2, load_kv(j), dead_ref[g], mask)
                acc_ref[g] += contrib
                dead_ref[g] = dead
                return jnp.min(dead) < DEAD_LOG

            def cond(carry):
                j, alive = carry
                return jnp.logical_and(j >= 0, alive)

            def body(carry, step=step):
                j, _ = carry
                return j - 1, step(j, None)

            @pl.when(jnp.min(dead_ref[g]) < DEAD_LOG)
            def _(last_block=last_block, step=step, cond=cond, body=body):
                alive = step(last_block, t_idx >= ATTN_LAST_VISIT_ROWS)
                lax.while_loop(cond, body, (last_block - 1, alive))

    @pl.when(ig == 0)
    def _():
        pl.when(unrolled_visits(0))(lambda: remaining_visits(0))

    @pl.when(ig > 0)
    def _():
        first_block = ig * ATTN_Q_BLOCKS
        pl.when(unrolled_visits(first_block))(lambda: remaining_visits(first_block))

    for g in range(ATTN_Q_BLOCKS):
        o_ref[g * Q_BLOCK:(g + 1) * Q_BLOCK, :] = acc_ref[g].astype(o_ref.dtype)


def _attention(q, k, v):
    blocks, b, s, _ = q.shape
    group_rows = ATTN_Q_BLOCKS * Q_BLOCK
    assert ATTN_Q_BLOCKS >= ATTN_UNROLLED_VISITS - 1
    assert s % group_rows == 0 and q.shape[-1] == LANES
    assert 0 < ATTN_LAST_VISIT_ROWS <= Q_BLOCK and ATTN_LAST_VISIT_ROWS % BF16_SUBLANES == 0
    q_spec = pl.BlockSpec((None, None, group_rows, LANES), lambda bi, c, i: (c, bi, i, 0))
    kv_spec = pl.BlockSpec((None, None, s, LANES), lambda bi, c, i: (c, bi, 0, 0))
    return pl.pallas_call(
        _attn_kernel,
        grid=(b, blocks, s // group_rows),
        in_specs=[q_spec, kv_spec, kv_spec],
        out_specs=q_spec,
        out_shape=jax.ShapeDtypeStruct(q.shape, _BF16),
        scratch_shapes=[pltpu.VMEM((ATTN_Q_BLOCKS, Q_BLOCK, LANES), _F32),
                        pltpu.VMEM((ATTN_Q_BLOCKS, HEADS_PER_STEP * Q_BLOCK, 1), _F32)],
        compiler_params=pltpu.CompilerParams(
            dimension_semantics=("arbitrary", "arbitrary", "arbitrary"),
            vmem_limit_bytes=VMEM_LIMIT_BYTES),
        name="attn",
    )(q, k, v)


def kernel(x, ffn1_norm, ffn1_w_in, ffn1_w_out, ffn2_norm, ffn2_w_in, ffn2_w_out, pool_norm,
           pool_w, pool_scale, kv_norm, w_kv, k_gain, attn_norm, w_q, q_gain, w_o):
    b, s, d = x.shape
    assert d == N_HEADS * HEAD_DIM
    h = x.reshape(b * s, d)
    w1_in, w1_out, w2_in, w2_out = (
        w.astype(_BF16) for w in (ffn1_w_in, ffn1_w_out, ffn2_w_in, ffn2_w_out))

    h = _ffn(h, ffn1_norm[0], w1_in, w1_out, layer=0)
    h = _pool_ffn(h, s, pool_norm[0], pool_w[0], pool_scale[0], ffn2_norm[0], w2_in, w2_out,
                  layer=0)

    k, v = _head_proj(h, kv_norm, w_kv, k_gain, d_norm=d)

    h = _ffn(h, ffn1_norm[1], w1_in, w1_out, layer=1)
    (q,) = _head_proj(h, attn_norm[0], w_q[0], q_gain[0] * (HEAD_DIM ** -0.5), d_norm=d)
    per_batch = lambda a: a.reshape(a.shape[0], b, s, LANES)
    o = _attention(per_batch(q), per_batch(k), per_batch(v))
    h = _proj_ffn(h, o.reshape(o.shape[0], b * s, LANES), w_o[0], ffn2_norm[1], w2_in, w2_out,
                  layer=1)
    return h.reshape(b, s, d)
```

```python
import functools

import jax
import jax.numpy as jnp
from jax import lax
from jax.experimental import pallas as pl
from jax.experimental.pallas import tpu as pltpu

N_HEADS = 16
HEAD_DIM = 64
POOL_WINDOWS = (2, 4, 8, 16)
MAX_WINDOW = max(POOL_WINDOWS)
Q_BLOCK = 128
RMS_EPS = 1e-6
FFN_RES_WEIGHT = 0.5

LANES = 128
BF16_SUBLANES = 16
MXU_DIM = 256
VMEM_LIMIT_BYTES = 56 * 1024 * 1024

FFN_ROWS = 1024
FFN_CHUNK = MXU_DIM
POOL_ROW_BLOCKS = 2
PROJ_ROWS = 1024
HEADS_PER_STEP = LANES // HEAD_DIM
ATTN_Q_BLOCKS = 16
ATTN_UNROLLED_VISITS = 3
ATTN_LAST_VISIT_ROWS = 32

DEAD_LOG = 105.0
LOG2_E = 1.4426950408889634

_BF16 = jnp.bfloat16
_F32 = jnp.float32


def _resident(shape):
    zeros = (0,) * len(shape)
    return pl.BlockSpec(shape, lambda *_: zeros, pipeline_mode=pl.Buffered(1))


def _rms_norm(x, gain):
    ms = jnp.mean(x * x, axis=-1, keepdims=True)
    return x * lax.rsqrt(ms + RMS_EPS) * gain


def _dot(a, b):
    return jnp.dot(a, b, preferred_element_type=_F32)


def _ffn_update(x, g_ref, win_ref, wout_ref, o_ref, side_work=()):
    d_ff = wout_ref.shape[0]
    n_chunks = d_ff // FFN_CHUNK
    emit_after = {(k + 1) * n_chunks // (len(side_work) + 1) - 1: work
                  for k, work in enumerate(side_work)}
    xn = _rms_norm(x, g_ref[...]).astype(_BF16)
    acc = x * (1.0 / FFN_RES_WEIGHT)
    for c in range(n_chunks):
        lo = c * FFN_CHUNK
        gate = _dot(xn, win_ref[:, lo:lo + FFN_CHUNK])
        up = _dot(xn, win_ref[:, d_ff + lo:d_ff + lo + FFN_CHUNK])
        act = (gate * jax.nn.sigmoid(gate) * up).astype(_BF16)
        acc = acc + _dot(act, wout_ref[lo:lo + FFN_CHUNK, :])
        if c in emit_after:
            emit_after[c]()
    o_ref[...] = FFN_RES_WEIGHT * acc


def _pool_pieces(x_ref, halo_ref, first_of_seq, g_ref, w_ref, scale_ref, o_ref):
    group_w = w_ref.shape[1]
    block = x_ref.shape[0] // POOL_ROW_BLOCKS
    shared = {}

    def normed(rb):
        if rb not in shared:
            gain = g_ref[...]
            if rb == 0:
                before = jnp.where(first_of_seq, 0.0, _rms_norm(halo_ref[...], gain))
                shared[rb] = jnp.concatenate([before, _rms_norm(x_ref[:block], gain)], axis=0)
            else:
                shared[rb] = _rms_norm(x_ref[rb * block - MAX_WINDOW:(rb + 1) * block], gain)
        return shared[rb]

    def piece(rb, g, w):
        ext = normed(rb)
        cols = slice(g * group_w, (g + 1) * group_w)
        rows = slice(rb * block, (rb + 1) * block)
        s = ext[:, cols]
        shift = 1
        while shift < w:
            s = s + pltpu.roll(s, shift, axis=0)
            shift *= 2
        if rb == 0:
            row = lax.broadcasted_iota(jnp.int32, (block, 1), 0)
            cnt = jnp.where(first_of_seq, jnp.minimum(row + 1, w), w)
            mean = s[MAX_WINDOW:] * (1.0 / cnt.astype(_F32))
        else:
            mean = s[MAX_WINDOW:] * (1.0 / w)
        y = _dot((mean - ext[MAX_WINDOW:, cols]).astype(_BF16), w_ref[g])
        o_ref[rows, cols] = x_ref[rows, cols] + y * scale_ref[:, cols]

    return [functools.partial(piece, rb, g, w)
            for rb in range(POOL_ROW_BLOCKS) for g, w in enumerate(POOL_WINDOWS)]


def _ffn_kernel(x_ref, g_ref, win_ref, wout_ref, o_ref):
    _ffn_update(x_ref[...], g_ref, win_ref, wout_ref, o_ref)


def _proj_ffn_kernel(x_ref, a_ref, wp_ref, g_ref, win_ref, wout_ref, o_ref):
    a = jnp.concatenate([a_ref[c] for c in range(a_ref.shape[0])], axis=1)
    x = x_ref[...] + _dot(a, wp_ref[...])
    _ffn_update(x, g_ref, win_ref, wout_ref, o_ref)


def _pool_ffn_kernel(x_ref, halo_ref, pg_ref, pw_ref, ps_ref, g_ref, win_ref, wout_ref, o_ref,
                     pooled_ref, *, tiles_per_seq):
    i = pl.program_id(0)
    tile = jnp.minimum(i, pl.num_programs(0) - 2)
    first_of_seq = tile % tiles_per_seq == 0

    def pool_pieces():
        return _pool_pieces(x_ref, halo_ref, first_of_seq, pg_ref, pw_ref, ps_ref, pooled_ref)

    @pl.when(i == 0)
    def _():
        for piece in pool_pieces():
            piece()

    @pl.when(i > 0)
    def _():
        _ffn_update(pooled_ref[...], g_ref, win_ref, wout_ref, o_ref, side_work=pool_pieces())


def _layer_resident(stacked, layer):
    index = (layer,) + (0,) * (stacked.ndim - 1)
    return pl.BlockSpec((None,) + stacked.shape[1:], lambda *_: index,
                        pipeline_mode=pl.Buffered(1))


def _ffn_call(body, name, t, d, steps, row_maps, specs, args, gain, w_in, w_out, layer,
              scratch=()):
    d_ff = w_out.shape[1]
    assert t % FFN_ROWS == 0 and d_ff % FFN_CHUNK == 0 and w_in.shape[1:] == (d, 2 * d_ff)
    assert w_in.dtype == _BF16 and w_out.dtype == _BF16
    ffn_specs = [_resident((1, d)), _layer_resident(w_in, layer), _layer_resident(w_out, layer)]
    return pl.pallas_call(
        body,
        grid=(steps,),
        in_specs=specs + ffn_specs,
        out_specs=pl.BlockSpec((FFN_ROWS, d), row_maps),
        out_shape=jax.ShapeDtypeStruct((t, d), _F32),
        scratch_shapes=list(scratch),
        compiler_params=pltpu.CompilerParams(
            dimension_semantics=("arbitrary",), vmem_limit_bytes=VMEM_LIMIT_BYTES),
        name=name,
    )(*args, gain.reshape(1, d), w_in, w_out)


def _ffn(h, gain, w_in, w_out, layer):
    t, d = h.shape
    rows = lambda i: (i, 0)
    return _ffn_call(_ffn_kernel, "ffn", t, d, t // FFN_ROWS, rows,
                     [pl.BlockSpec((FFN_ROWS, d), rows)], (h,), gain, w_in, w_out, layer)


def _proj_ffn(h, mixer_out, w_proj, gain, w_in, w_out, layer):
    t, d = h.shape
    blocks = mixer_out.shape[0]
    assert mixer_out.shape == (blocks, t, LANES) and w_proj.shape == (blocks * LANES, d)
    rows = lambda i: (i, 0)
    specs = [pl.BlockSpec((FFN_ROWS, d), rows),
             pl.BlockSpec((blocks, FFN_ROWS, LANES), lambda i: (0, i, 0)),
             _resident(w_proj.shape)]
    return _ffn_call(_proj_ffn_kernel, "proj_ffn", t, d, t // FFN_ROWS, rows, specs,
                     (h, mixer_out, w_proj.astype(_BF16)), gain, w_in, w_out, layer)


def _pool_ffn(h, seq_len, pool_gain, pool_w, pool_scale, gain, w_in, w_out, layer):
    t, d = h.shape
    assert seq_len % FFN_ROWS == 0 and FFN_ROWS % MAX_WINDOW == 0 and t % seq_len == 0
    tiles = t // FFN_ROWS
    halo_blocks = FFN_ROWS // MAX_WINDOW
    pooled_tile = lambda i: jnp.minimum(i, tiles - 1)
    specs = [pl.BlockSpec((FFN_ROWS, d), lambda i: (pooled_tile(i), 0)),
             pl.BlockSpec((MAX_WINDOW, d),
                          lambda i: (jnp.maximum(pooled_tile(i) * halo_blocks - 1, 0), 0)),
             _resident((1, d)), _resident(pool_w.shape), _resident((1, d))]
    args = (h, h, pool_gain.reshape(1, d), pool_w.astype(_BF16), pool_scale.reshape(1, d))
    return _ffn_call(
        functools.partial(_pool_ffn_kernel, tiles_per_seq=seq_len // FFN_ROWS), "pool_ffn",
        t, d, tiles + 1, lambda i: (jnp.maximum(i - 1, 0), 0), specs, args, gain, w_in, w_out,
        layer, scratch=[pltpu.VMEM((FFN_ROWS, d), _F32)])


def _head_proj_kernel(x_ref, g_ref, w_ref, hg_ref, *o_refs, d_norm):
    xn = _rms_norm(x_ref[...], g_ref[...]).astype(_BF16)
    y = _dot(xn, w_ref[...])
    first_head = lax.broadcasted_iota(jnp.int32, (y.shape[0], LANES), 1) < HEAD_DIM
    for c in range(d_norm // LANES):
        cols = slice(c * LANES, (c + 1) * LANES)
        yc = y[:, cols]
        sq = yc * yc
        ms = [jnp.sum(jnp.where(keep, sq, 0.0), axis=-1, keepdims=True) * (1.0 / HEAD_DIM)
              for keep in (first_head, ~first_head)]
        r = jnp.where(first_head, lax.rsqrt(ms[0] + RMS_EPS), lax.rsqrt(ms[1] + RMS_EPS))
        o_refs[0][c] = (yc * r * hg_ref[:, cols]).astype(o_refs[0].dtype)
    if len(o_refs) > 1:
        for c in range(o_refs[1].shape[0]):
            o_refs[1][c] = y[:, d_norm + c * LANES:d_norm + (c + 1) * LANES].astype(
                o_refs[1].dtype)


def _head_proj(h, gain, w, head_gain, d_norm):
    t, d = h.shape
    d_out = w.shape[1]
    assert t % PROJ_ROWS == 0 and HEADS_PER_STEP == 2
    assert d_norm % LANES == 0 and (d_out - d_norm) % LANES == 0
    hg = jnp.tile(head_gain, d_norm // HEAD_DIM).reshape(1, d_norm)
    row_spec = pl.BlockSpec((PROJ_ROWS, d), lambda i: (i, 0))
    widths = [d_norm] + ([d_out - d_norm] if d_out > d_norm else [])
    out_shapes = [jax.ShapeDtypeStruct((n // LANES, t, LANES), _BF16) for n in widths]
    out_specs = [pl.BlockSpec((n // LANES, PROJ_ROWS, LANES), lambda i: (0, i, 0)) for n in widths]
    return pl.pallas_call(
        functools.partial(_head_proj_kernel, d_norm=d_norm),
        grid=(t // PROJ_ROWS,),
        in_specs=[row_spec, _resident((1, d)), _resident(w.shape), _resident((1, d_norm))],
        out_specs=out_specs,
        out_shape=out_shapes,
        compiler_params=pltpu.CompilerParams(
            dimension_semantics=("arbitrary",), vmem_limit_bytes=VMEM_LIMIT_BYTES),
        name="head_proj",
    )(h, gain.reshape(1, d), w.astype(_BF16), hg)


def _attn_kernel(q_ref, k_ref, v_ref, o_ref, acc_ref, dead_ref):
    ig = pl.program_id(2)

    src = lax.broadcasted_iota(jnp.int32, (Q_BLOCK, Q_BLOCK), 0)
    dst = lax.broadcasted_iota(jnp.int32, (Q_BLOCK, Q_BLOCK), 1)
    suffix_mat = (src > dst).astype(_BF16)

    t_idx = lax.broadcasted_iota(jnp.int32, (2 * Q_BLOCK, Q_BLOCK), 0) % Q_BLOCK
    s_idx = lax.broadcasted_iota(jnp.int32, (2 * Q_BLOCK, Q_BLOCK), 1)
    causal = s_idx < t_idx

    def stack_heads(x):
        zero = jnp.zeros_like(x)
        keep = lax.broadcasted_iota(jnp.int32, x.shape, 1) < HEAD_DIM
        return jnp.concatenate([jnp.where(keep, x, zero), jnp.where(keep, zero, x)], axis=0)

    def load_kv(j):
        start = j * Q_BLOCK
        if not isinstance(start, int):
            start = pl.multiple_of(start, Q_BLOCK)
        rows = pl.ds(start, Q_BLOCK)
        return k_ref[rows, :], stack_heads(v_ref[rows, :])

    def score(q2, kj):
        return lax.dot_general(q2, kj, (((1,), (1,)), ((), ())), preferred_element_type=_F32)

    def log_terms(z, mask):
        y = jnp.maximum(z, 0.0) + jnp.log(1.0 + jnp.exp2(jnp.abs(z) * (-LOG2_E)))
        return (y if mask is None else jnp.where(mask, y, 0.0)), z - y

    def later_sums(y):
        return _dot(y.astype(_BF16), suffix_mat)

    def weights(log_sig, later, dead, mask):
        drop = later if dead is None else later + dead
        a = jnp.exp(log_sig - drop)
        if mask is not None:
            a = jnp.where(mask, a, 0.0)
        a = a.astype(_BF16)
        rows = a.shape[0] // HEADS_PER_STEP
        return jnp.concatenate([a[:rows], a[rows:]], axis=1)

    def first_rows(x, rows):
        return x if rows == Q_BLOCK else jnp.concatenate(
            [x[:rows], x[Q_BLOCK:Q_BLOCK + rows]], axis=0)

    def add_first_rows(x, part, rows):
        if rows == Q_BLOCK:
            return x + part
        pieces = []
        for h in range(x.shape[0] // Q_BLOCK):
            pieces += [x[h * Q_BLOCK:h * Q_BLOCK + rows] + part[h * rows:(h + 1) * rows],
                       x[h * Q_BLOCK + rows:(h + 1) * Q_BLOCK]]
        return jnp.concatenate(pieces, axis=0)

    def visit_rows(back):
        return ATTN_LAST_VISIT_ROWS if back == ATTN_UNROLLED_VISITS - 1 else Q_BLOCK

    def rows_of(parts):
        return parts[0] if len(parts) == 1 else jnp.concatenate(parts, axis=0)

    def split_rows(x, visits, rows_per_query):
        out, at = {}, 0
        for v in visits:
            n = visit_rows(v[1]) * rows_per_query
            out[v] = x[at:at + n]
            at += n
        return out

    def unrolled_visits(first_block):
        users = {}
        for g in range(ATTN_Q_BLOCKS):
            for back in range(ATTN_UNROLLED_VISITS):
                users.setdefault(g - back, []).append((g, back))
        offsets = sorted(users, reverse=True)
        order = [v for e in offsets for v in users[e]]
        q = [q_ref[g * Q_BLOCK:(g + 1) * Q_BLOCK, :] for g in range(ATTN_Q_BLOCKS)]
        q2 = {v: stack_heads(q[v[0]][:visit_rows(v[1])]) for v in order}
        kv = {e: load_kv(jnp.maximum(first_block + e, 0)) for e in offsets}

        def mask_of(v):
            g, back = v
            if back == 0:
                return causal
            return None if g - back >= 0 else first_block + g - back >= 0

        z = {}
        for e in offsets:
            zc = score(rows_of([q2[v] for v in users[e]]), kv[e][0])
            z.update(split_rows(zc, users[e], HEADS_PER_STEP))
        y, log_sig = {}, {}
        for v in order:
            y[v], log_sig[v] = log_terms(z[v], mask_of(v))
        later = split_rows(later_sums(rows_of([y[v] for v in order])), order, HEADS_PER_STEP)
        total = {v: jnp.sum(y[v], axis=-1, keepdims=True) for v in order}

        a_wide, dead_end = {}, {}
        for g in range(ATTN_Q_BLOCKS):
            dead = None
            for back in range(ATTN_UNROLLED_VISITS):
                v = (g, back)
                rows = visit_rows(back)
                a_wide[v] = weights(log_sig[v], later[v],
                                    None if dead is None else first_rows(dead, rows), mask_of(v))
                dead = total[v] if dead is None else add_first_rows(dead, total[v], rows)
            dead_end[g] = dead

        acc = {}
        for e in offsets:
            c = split_rows(_dot(rows_of([a_wide[v] for v in users[e]]), kv[e][1]), users[e], 1)
            for v in users[e]:
                g = v[0]
                acc[g] = c[v] if g not in acc else add_first_rows(acc[g], c[v], visit_rows(v[1]))

        most_alive = None
        for g in range(ATTN_Q_BLOCKS):
            acc_ref[g] = acc[g]
            dead_ref[g] = dead_end[g]
            g_min = jnp.min(dead_end[g])
            most_alive = g_min if most_alive is None else jnp.minimum(most_alive, g_min)
        return most_alive < DEAD_LOG

    def visit(q2, kv, dead, mask):
        y, log_sig = log_terms(score(q2, kv[0]), mask)
        a = weights(log_sig, later_sums(y), dead, mask)
        return _dot(a, kv[1]), dead + jnp.sum(y, axis=-1, keepdims=True)

    def remaining_visits(first_block):
        def per_query_block(g, carry):
            last_block = first_block + g - (ATTN_UNROLLED_VISITS - 1)
            q_rows = pl.ds(pl.multiple_of(g * Q_BLOCK, Q_BLOCK), Q_BLOCK)

            def step(j, mask):
                contrib, dead = visit(stack_heads(q_ref[q_rows, :]), load_kv(j), dead_ref[g], mask)
                acc_ref[g] += contrib
                dead_ref[g] = dead
                return jnp.min(dead) < DEAD_LOG

            def cond(state):
                j, alive = state
                return jnp.logical_and(j >= 0, alive)

            def body(state):
                j, _ = state
                return j - 1, step(j, None)

            @pl.when(jnp.logical_and(last_block >= 0, jnp.min(dead_ref[g]) < DEAD_LOG))
            def _():
                alive = step(last_block, t_idx >= ATTN_LAST_VISIT_ROWS)
                lax.while_loop(cond, body, (last_block - 1, alive))

            return carry

        lax.fori_loop(0, ATTN_Q_BLOCKS, per_query_block, 0)

    first_block = ig * ATTN_Q_BLOCKS
    pl.when(unrolled_visits(first_block))(lambda: remaining_visits(first_block))

    for g in range(ATTN_Q_BLOCKS):
        o_ref[g * Q_BLOCK:(g + 1) * Q_BLOCK, :] = acc_ref[g].astype(o_ref.dtype)


def _attention(q, k, v):
    blocks, b, s, _ = q.shape
    group_rows = ATTN_Q_BLOCKS * Q_BLOCK
    assert ATTN_Q_BLOCKS >= ATTN_UNROLLED_VISITS - 1
    assert s % group_rows == 0 and q.shape[-1] == LANES
    assert 0 < ATTN_LAST_VISIT_ROWS <= Q_BLOCK and ATTN_LAST_VISIT_ROWS % BF16_SUBLANES == 0
    q_spec = pl.BlockSpec((None, None, group_rows, LANES), lambda bi, c, i: (c, bi, i, 0))
    kv_spec = pl.BlockSpec((None, None, s, LANES), lambda bi, c, i: (c, bi, 0, 0))
    return pl.pallas_call(
        _attn_kernel,
        grid=(b, blocks, s // group_rows),
        in_specs=[q_spec, kv_spec, kv_spec],
        out_specs=q_spec,
        out_shape=jax.ShapeDtypeStruct(q.shape, _BF16),
        scratch_shapes=[pltpu.VMEM((ATTN_Q_BLOCKS, Q_BLOCK, LANES), _F32),
                        pltpu.VMEM((ATTN_Q_BLOCKS, HEADS_PER_STEP * Q_BLOCK, 1), _F32)],
        compiler_params=pltpu.CompilerParams(
            dimension_semantics=("arbitrary", "arbitrary", "arbitrary"),
            vmem_limit_bytes=VMEM_LIMIT_BYTES),
        name="attn",
    )(q, k, v)


def kernel(x, ffn1_norm, ffn1_w_in, ffn1_w_out, ffn2_norm, ffn2_w_in, ffn2_w_out, pool_norm,
           pool_w, pool_scale, kv_norm, w_kv, k_gain, attn_norm, w_q, q_gain, w_o):
    b, s, d = x.shape
    assert d == N_HEADS * HEAD_DIM
    h = x.reshape(b * s, d)
    w1_in, w1_out, w2_in, w2_out = (
        w.astype(_BF16) for w in (ffn1_w_in, ffn1_w_out, ffn2_w_in, ffn2_w_out))

    h = _ffn(h, ffn1_norm[0], w1_in, w1_out, layer=0)
    h = _pool_ffn(h, s, pool_norm[0], pool_w[0], pool_scale[0], ffn2_norm[0], w2_in, w2_out,
                  layer=0)

    k, v = _head_proj(h, kv_norm, w_kv, k_gain, d_norm=d)

    h = _ffn(h, ffn1_norm[1], w1_in, w1_out, layer=1)
    (q,) = _head_proj(h, attn_norm[0], w_q[0], q_gain[0] * (HEAD_DIM ** -0.5), d_norm=d)
    per_batch = lambda a: a.reshape(a.shape[0], b, s, LANES)
    o = _attention(per_batch(q), per_batch(k), per_batch(v))
    h = _proj_ffn(h, o.reshape(o.shape[0], b * s, LANES), w_o[0], ffn2_norm[1], w2_in, w2_out,
                  layer=1)
    return h.reshape(b, s, d)
```

```python
import functools

import jax
import jax.numpy as jnp
from jax import lax
from jax.experimental import pallas as pl
from jax.experimental.pallas import tpu as pltpu

N_HEADS = 16
HEAD_DIM = 64
POOL_WINDOWS = (2, 4, 8, 16)
MAX_WINDOW = max(POOL_WINDOWS)
Q_BLOCK = 128
RMS_EPS = 1e-6
FFN_RES_WEIGHT = 0.5

LANES = 128
BF16_SUBLANES = 16
MXU_DIM = 256
VMEM_LIMIT_BYTES = 56 * 1024 * 1024

FFN_ROWS = 1024
FFN_CHUNK = MXU_DIM
POOL_ROW_BLOCKS = 2
PROJ_ROWS = 1024
HEADS_PER_STEP = LANES // HEAD_DIM
ATTN_Q_BLOCKS = 32
ATTN_UNROLLED_VISITS = 3
ATTN_LAST_VISIT_ROWS = 32

DEAD_LOG = 105.0
LOG2_E = 1.4426950408889634

_BF16 = jnp.bfloat16
_F32 = jnp.float32


def _resident(shape):
    zeros = (0,) * len(shape)
    return pl.BlockSpec(shape, lambda *_: zeros, pipeline_mode=pl.Buffered(1))


def _rms_norm(x, gain):
    ms = jnp.mean(x * x, axis=-1, keepdims=True)
    return x * lax.rsqrt(ms + RMS_EPS) * gain


def _dot(a, b):
    return jnp.dot(a, b, preferred_element_type=_F32)


def _ffn_update(x, g_ref, win_ref, wout_ref, o_ref, side_work=()):
    d_ff = wout_ref.shape[0]
    n_chunks = d_ff // FFN_CHUNK
    emit_after = {(k + 1) * n_chunks // (len(side_work) + 1) - 1: work
                  for k, work in enumerate(side_work)}
    xn = _rms_norm(x, g_ref[...]).astype(_BF16)
    acc = x * (1.0 / FFN_RES_WEIGHT)
    for c in range(n_chunks):
        lo = c * FFN_CHUNK
        gate = _dot(xn, win_ref[:, lo:lo + FFN_CHUNK])
        up = _dot(xn, win_ref[:, d_ff + lo:d_ff + lo + FFN_CHUNK])
        act = (gate * jax.nn.sigmoid(gate) * up).astype(_BF16)
        acc = acc + _dot(act, wout_ref[lo:lo + FFN_CHUNK, :])
        if c in emit_after:
            emit_after[c]()
    o_ref[...] = FFN_RES_WEIGHT * acc


def _pool_pieces(x_ref, halo_ref, first_of_seq, g_ref, w_ref, scale_ref, o_ref):
    group_w = w_ref.shape[1]
    block = x_ref.shape[0] // POOL_ROW_BLOCKS
    shared = {}

    def normed(rb):
        if rb not in shared:
            gain = g_ref[...]
            if rb == 0:
                before = jnp.where(first_of_seq, 0.0, _rms_norm(halo_ref[...], gain))
                shared[rb] = jnp.concatenate([before, _rms_norm(x_ref[:block], gain)], axis=0)
            else:
                shared[rb] = _rms_norm(x_ref[rb * block - MAX_WINDOW:(rb + 1) * block], gain)
        return shared[rb]

    def piece(rb, g, w):
        ext = normed(rb)
        cols = slice(g * group_w, (g + 1) * group_w)
        rows = slice(rb * block, (rb + 1) * block)
        s = ext[:, cols]
        shift = 1
        while shift < w:
            s = s + pltpu.roll(s, shift, axis=0)
            shift *= 2
        if rb == 0:
            row = lax.broadcasted_iota(jnp.int32, (block, 1), 0)
            cnt = jnp.where(first_of_seq, jnp.minimum(row + 1, w), w)
            mean = s[MAX_WINDOW:] * (1.0 / cnt.astype(_F32))
        else:
            mean = s[MAX_WINDOW:] * (1.0 / w)
        y = _dot((mean - ext[MAX_WINDOW:, cols]).astype(_BF16), w_ref[g])
        o_ref[rows, cols] = x_ref[rows, cols] + y * scale_ref[:, cols]

    return [functools.partial(piece, rb, g, w)
            for rb in range(POOL_ROW_BLOCKS) for g, w in enumerate(POOL_WINDOWS)]


def _ffn_kernel(x_ref, g_ref, win_ref, wout_ref, o_ref):
    _ffn_update(x_ref[...], g_ref, win_ref, wout_ref, o_ref)


def _proj_ffn_kernel(x_ref, a_ref, wp_ref, g_ref, win_ref, wout_ref, o_ref):
    a = jnp.concatenate([a_ref[c] for c in range(a_ref.shape[0])], axis=1)
    x = x_ref[...] + _dot(a, wp_ref[...])
    _ffn_update(x, g_ref, win_ref, wout_ref, o_ref)


def _pool_ffn_kernel(x_ref, halo_ref, pg_ref, pw_ref, ps_ref, g_ref, win_ref, wout_ref, o_ref,
                     pooled_ref, *, tiles_per_seq):
    i = pl.program_id(0)
    tile = jnp.minimum(i, pl.num_programs(0) - 2)
    first_of_seq = tile % tiles_per_seq == 0

    def pool_pieces():
        return _pool_pieces(x_ref, halo_ref, first_of_seq, pg_ref, pw_ref, ps_ref, pooled_ref)

    @pl.when(i == 0)
    def _():
        for piece in pool_pieces():
            piece()

    @pl.when(i > 0)
    def _():
        _ffn_update(pooled_ref[...], g_ref, win_ref, wout_ref, o_ref, side_work=pool_pieces())


def _layer_resident(stacked, layer):
    index = (layer,) + (0,) * (stacked.ndim - 1)
    return pl.BlockSpec((None,) + stacked.shape[1:], lambda *_: index,
                        pipeline_mode=pl.Buffered(1))


def _ffn_call(body, name, t, d, steps, row_maps, specs, args, gain, w_in, w_out, layer,
              scratch=()):
    d_ff = w_out.shape[1]
    assert t % FFN_ROWS == 0 and d_ff % FFN_CHUNK == 0 and w_in.shape[1:] == (d, 2 * d_ff)
    assert w_in.dtype == _BF16 and w_out.dtype == _BF16
    ffn_specs = [_resident((1, d)), _layer_resident(w_in, layer), _layer_resident(w_out, layer)]
    return pl.pallas_call(
        body,
        grid=(steps,),
        in_specs=specs + ffn_specs,
        out_specs=pl.BlockSpec((FFN_ROWS, d), row_maps),
        out_shape=jax.ShapeDtypeStruct((t, d), _F32),
        scratch_shapes=list(scratch),
        compiler_params=pltpu.CompilerParams(
            dimension_semantics=("arbitrary",), vmem_limit_bytes=VMEM_LIMIT_BYTES),
        name=name,
    )(*args, gain.reshape(1, d), w_in, w_out)


def _ffn(h, gain, w_in, w_out, layer):
    t, d = h.shape
    rows = lambda i: (i, 0)
    return _ffn_call(_ffn_kernel, "ffn", t, d, t // FFN_ROWS, rows,
                     [pl.BlockSpec((FFN_ROWS, d), rows)], (h,), gain, w_in, w_out, layer)


def _proj_ffn(h, mixer_out, w_proj, gain, w_in, w_out, layer):
    t, d = h.shape
    blocks = mixer_out.shape[0]
    assert mixer_out.shape == (blocks, t, LANES) and w_proj.shape == (blocks * LANES, d)
    rows = lambda i: (i, 0)
    specs = [pl.BlockSpec((FFN_ROWS, d), rows),
             pl.BlockSpec((blocks, FFN_ROWS, LANES), lambda i: (0, i, 0)),
             _resident(w_proj.shape)]
    return _ffn_call(_proj_ffn_kernel, "proj_ffn", t, d, t // FFN_ROWS, rows, specs,
                     (h, mixer_out, w_proj.astype(_BF16)), gain, w_in, w_out, layer)


def _pool_ffn(h, seq_len, pool_gain, pool_w, pool_scale, gain, w_in, w_out, layer):
    t, d = h.shape
    assert seq_len % FFN_ROWS == 0 and FFN_ROWS % MAX_WINDOW == 0 and t % seq_len == 0
    tiles = t // FFN_ROWS
    halo_blocks = FFN_ROWS // MAX_WINDOW
    pooled_tile = lambda i: jnp.minimum(i, tiles - 1)
    specs = [pl.BlockSpec((FFN_ROWS, d), lambda i: (pooled_tile(i), 0)),
             pl.BlockSpec((MAX_WINDOW, d),
                          lambda i: (jnp.maximum(pooled_tile(i) * halo_blocks - 1, 0), 0)),
             _resident((1, d)), _resident(pool_w.shape), _resident((1, d))]
    args = (h, h, pool_gain.reshape(1, d), pool_w.astype(_BF16), pool_scale.reshape(1, d))
    return _ffn_call(
        functools.partial(_pool_ffn_kernel, tiles_per_seq=seq_len // FFN_ROWS), "pool_ffn",
        t, d, tiles + 1, lambda i: (jnp.maximum(i - 1, 0), 0), specs, args, gain, w_in, w_out,
        layer, scratch=[pltpu.VMEM((FFN_ROWS, d), _F32)])


def _head_proj_kernel(x_ref, g_ref, w_ref, hg_ref, *o_refs, d_norm):
    xn = _rms_norm(x_ref[...], g_ref[...]).astype(_BF16)
    y = _dot(xn, w_ref[...])
    first_head = lax.broadcasted_iota(jnp.int32, (y.shape[0], LANES), 1) < HEAD_DIM
    for c in range(d_norm // LANES):
        cols = slice(c * LANES, (c + 1) * LANES)
        yc = y[:, cols]
        sq = yc * yc
        ms = [jnp.sum(jnp.where(keep, sq, 0.0), axis=-1, keepdims=True) * (1.0 / HEAD_DIM)
              for keep in (first_head, ~first_head)]
        r = jnp.where(first_head, lax.rsqrt(ms[0] + RMS_EPS), lax.rsqrt(ms[1] + RMS_EPS))
        o_refs[0][c] = (yc * r * hg_ref[:, cols]).astype(o_refs[0].dtype)
    if len(o_refs) > 1:
        for c in range(o_refs[1].shape[0]):
            o_refs[1][c] = y[:, d_norm + c * LANES:d_norm + (c + 1) * LANES].astype(
                o_refs[1].dtype)


def _head_proj(h, gain, w, head_gain, d_norm):
    t, d = h.shape
    d_out = w.shape[1]
    assert t % PROJ_ROWS == 0 and HEADS_PER_STEP == 2
    assert d_norm % LANES == 0 and (d_out - d_norm) % LANES == 0
    hg = jnp.tile(head_gain, d_norm // HEAD_DIM).reshape(1, d_norm)
    row_spec = pl.BlockSpec((PROJ_ROWS, d), lambda i: (i, 0))
    widths = [d_norm] + ([d_out - d_norm] if d_out > d_norm else [])
    out_shapes = [jax.ShapeDtypeStruct((n // LANES, t, LANES), _BF16) for n in widths]
    out_specs = [pl.BlockSpec((n // LANES, PROJ_ROWS, LANES), lambda i: (0, i, 0)) for n in widths]
    return pl.pallas_call(
        functools.partial(_head_proj_kernel, d_norm=d_norm),
        grid=(t // PROJ_ROWS,),
        in_specs=[row_spec, _resident((1, d)), _resident(w.shape), _resident((1, d_norm))],
        out_specs=out_specs,
        out_shape=out_shapes,
        compiler_params=pltpu.CompilerParams(
            dimension_semantics=("arbitrary",), vmem_limit_bytes=VMEM_LIMIT_BYTES),
        name="head_proj",
    )(h, gain.reshape(1, d), w.astype(_BF16), hg)


def _attn_kernel(q_ref, k_ref, v_ref, o_ref, acc_ref, dead_ref):
    ig = pl.program_id(2)

    src = lax.broadcasted_iota(jnp.int32, (Q_BLOCK, Q_BLOCK), 0)
    dst = lax.broadcasted_iota(jnp.int32, (Q_BLOCK, Q_BLOCK), 1)
    suffix_mat = (src > dst).astype(_BF16)

    t_idx = lax.broadcasted_iota(jnp.int32, (2 * Q_BLOCK, Q_BLOCK), 0) % Q_BLOCK
    s_idx = lax.broadcasted_iota(jnp.int32, (2 * Q_BLOCK, Q_BLOCK), 1)
    causal = s_idx < t_idx

    def stack_heads(x):
        zero = jnp.zeros_like(x)
        keep = lax.broadcasted_iota(jnp.int32, x.shape, 1) < HEAD_DIM
        return jnp.concatenate([jnp.where(keep, x, zero), jnp.where(keep, zero, x)], axis=0)

    def load_kv(j):
        start = j * Q_BLOCK
        if not isinstance(start, int):
            start = pl.multiple_of(start, Q_BLOCK)
        rows = pl.ds(start, Q_BLOCK)
        return k_ref[rows, :], stack_heads(v_ref[rows, :])

    def score(q2, kj):
        return lax.dot_general(q2, kj, (((1,), (1,)), ((), ())), preferred_element_type=_F32)

    def log_terms(z, mask):
        y = jnp.maximum(z, 0.0) + jnp.log(1.0 + jnp.exp2(jnp.abs(z) * (-LOG2_E)))
        return (y if mask is None else jnp.where(mask, y, 0.0)), z - y

    def later_sums(y):
        return _dot(y.astype(_BF16), suffix_mat)

    def weights(log_sig, later, dead, mask):
        drop = later if dead is None else later + dead
        a = jnp.exp(log_sig - drop)
        if mask is not None:
            a = jnp.where(mask, a, 0.0)
        a = a.astype(_BF16)
        rows = a.shape[0] // HEADS_PER_STEP
        return jnp.concatenate([a[:rows], a[rows:]], axis=1)

    def first_rows(x, rows):
        return x if rows == Q_BLOCK else jnp.concatenate(
            [x[:rows], x[Q_BLOCK:Q_BLOCK + rows]], axis=0)

    def add_first_rows(x, part, rows):
        if rows == Q_BLOCK:
            return x + part
        pieces = []
        for h in range(x.shape[0] // Q_BLOCK):
            pieces += [x[h * Q_BLOCK:h * Q_BLOCK + rows] + part[h * rows:(h + 1) * rows],
                       x[h * Q_BLOCK + rows:(h + 1) * Q_BLOCK]]
        return jnp.concatenate(pieces, axis=0)

    def visit_rows(back):
        return ATTN_LAST_VISIT_ROWS if back == ATTN_UNROLLED_VISITS - 1 else Q_BLOCK

    def rows_of(parts):
        return parts[0] if len(parts) == 1 else jnp.concatenate(parts, axis=0)

    def split_rows(x, visits, rows_per_query):
        out, at = {}, 0
        for v in visits:
            n = visit_rows(v[1]) * rows_per_query
            out[v] = x[at:at + n]
            at += n
        return out

    def unrolled_visits(first_block):
        users = {}
        for g in range(ATTN_Q_BLOCKS):
            for back in range(ATTN_UNROLLED_VISITS):
                users.setdefault(g - back, []).append((g, back))
        offsets = sorted(users, reverse=True)
        order = [v for e in offsets for v in users[e]]
        q = [q_ref[g * Q_BLOCK:(g + 1) * Q_BLOCK, :] for g in range(ATTN_Q_BLOCKS)]
        q2 = {v: stack_heads(q[v[0]][:visit_rows(v[1])]) for v in order}
        kv = {e: load_kv(jnp.maximum(first_block + e, 0)) for e in offsets}

        def mask_of(v):
            g, back = v
            if back == 0:
                return causal
            return None if g - back >= 0 else first_block + g - back >= 0

        z = {}
        for e in offsets:
            zc = score(rows_of([q2[v] for v in users[e]]), kv[e][0])
            z.update(split_rows(zc, users[e], HEADS_PER_STEP))
        y, log_sig = {}, {}
        for v in order:
            y[v], log_sig[v] = log_terms(z[v], mask_of(v))
        later = split_rows(later_sums(rows_of([y[v] for v in order])), order, HEADS_PER_STEP)
        total = {v: jnp.sum(y[v], axis=-1, keepdims=True) for v in order}

        a_wide, dead_end = {}, {}
        for g in range(ATTN_Q_BLOCKS):
            dead = None
            for back in range(ATTN_UNROLLED_VISITS):
                v = (g, back)
                rows = visit_rows(back)
                a_wide[v] = weights(log_sig[v], later[v],
                                    None if dead is None else first_rows(dead, rows), mask_of(v))
                dead = total[v] if dead is None else add_first_rows(dead, total[v], rows)
            dead_end[g] = dead

        acc = {}
        for e in offsets:
            c = split_rows(_dot(rows_of([a_wide[v] for v in users[e]]), kv[e][1]), users[e], 1)
            for v in users[e]:
                g = v[0]
                acc[g] = c[v] if g not in acc else add_first_rows(acc[g], c[v], visit_rows(v[1]))

        most_alive = None
        for g in range(ATTN_Q_BLOCKS):
            acc_ref[g] = acc[g]
            dead_ref[g] = dead_end[g]
            g_min = jnp.min(dead_end[g])
            most_alive = g_min if most_alive is None else jnp.minimum(most_alive, g_min)
        return most_alive < DEAD_LOG

    def visit(q2, kv, dead, mask):
        y, log_sig = log_terms(score(q2, kv[0]), mask)
        a = weights(log_sig, later_sums(y), dead, mask)
        return _dot(a, kv[1]), dead + jnp.sum(y, axis=-1, keepdims=True)

    def remaining_visits(first_block):
        def per_query_block(g, carry):
            last_block = first_block + g - (ATTN_UNROLLED_VISITS - 1)
            q_rows = pl.ds(pl.multiple_of(g * Q_BLOCK, Q_BLOCK), Q_BLOCK)

            def step(j, mask):
                contrib, dead = visit(stack_heads(q_ref[q_rows, :]), load_kv(j), dead_ref[g], mask)
                acc_ref[g] += contrib
                dead_ref[g] = dead
                return jnp.min(dead) < DEAD_LOG

            def cond(state):
                j, alive = state
                return jnp.logical_and(j >= 0, alive)

            def body(state):
                j, _ = state
                return j - 1, step(j, None)

            @pl.when(jnp.logical_and(last_block >= 0, jnp.min(dead_ref[g]) < DEAD_LOG))
            def _():
                alive = step(last_block, t_idx >= ATTN_LAST_VISIT_ROWS)
                lax.while_loop(cond, body, (last_block - 1, alive))

            return carry

        lax.fori_loop(0, ATTN_Q_BLOCKS, per_query_block, 0)

    first_block = ig * ATTN_Q_BLOCKS
    pl.when(unrolled_visits(first_block))(lambda: remaining_visits(first_block))

    for g in range(ATTN_Q_BLOCKS):
        o_ref[g * Q_BLOCK:(g + 1) * Q_BLOCK, :] = acc_ref[g].astype(o_ref.dtype)


def _attention(q, k, v):
    blocks, b, s, _ = q.shape
    group_rows = ATTN_Q_BLOCKS * Q_BLOCK
    assert ATTN_Q_BLOCKS >= ATTN_UNROLLED_VISITS - 1
    assert s % group_rows == 0 and q.shape[-1] == LANES
    assert 0 < ATTN_LAST_VISIT_ROWS <= Q_BLOCK and ATTN_LAST_VISIT_ROWS % BF16_SUBLANES == 0
    q_spec = pl.BlockSpec((None, None, group_rows, LANES), lambda bi, c, i: (c, bi, i, 0))
    kv_spec = pl.BlockSpec((None, None, s, LANES), lambda bi, c, i: (c, bi, 0, 0))
    return pl.pallas_call(
        _attn_kernel,
        grid=(b, blocks, s // group_rows),
        in_specs=[q_spec, kv_spec, kv_spec],
        out_specs=q_spec,
        out_shape=jax.ShapeDtypeStruct(q.shape, _BF16),
        scratch_shapes=[pltpu.VMEM((ATTN_Q_BLOCKS, Q_BLOCK, LANES), _F32),
                        pltpu.VMEM((ATTN_Q_BLOCKS, HEADS_PER_STEP * Q_BLOCK, 1), _F32)],
        compiler_params=pltpu.CompilerParams(
            dimension_semantics=("arbitrary", "arbitrary", "arbitrary"),
            vmem_limit_bytes=VMEM_LIMIT_BYTES),
        name="attn",
    )(q, k, v)


def kernel(x, ffn1_norm, ffn1_w_in, ffn1_w_out, ffn2_norm, ffn2_w_in, ffn2_w_out, pool_norm,
           pool_w, pool_scale, kv_norm, w_kv, k_gain, attn_norm, w_q, q_gain, w_o):
    b, s, d = x.shape
    assert d == N_HEADS * HEAD_DIM
    h = x.reshape(b * s, d)
    w1_in, w1_out, w2_in, w2_out = (
        w.astype(_BF16) for w in (ffn1_w_in, ffn1_w_out, ffn2_w_in, ffn2_w_out))

    h = _ffn(h, ffn1_norm[0], w1_in, w1_out, layer=0)
    h = _pool_ffn(h, s, pool_norm[0], pool_w[0], pool_scale[0], ffn2_norm[0], w2_in, w2_out,
                  layer=0)

    k, v = _head_proj(h, kv_norm, w_kv, k_gain, d_norm=d)

    h = _ffn(h, ffn1_norm[1], w1_in, w1_out, layer=1)
    (q,) = _head_proj(h, attn_norm[0], w_q[0], q_gain[0] * (HEAD_DIM ** -0.5), d_norm=d)
    per_batch = lambda a: a.reshape(a.shape[0], b, s, LANES)
    o = _attention(per_batch(q), per_batch(k), per_batch(v))
    h = _proj_ffn(h, o.reshape(o.shape[0], b * s, LANES), w_o[0], ffn2_norm[1], w2_in, w2_out,
                  layer=1)
    return h.reshape(b, s, d)
```

```python
import functools

import jax
import jax.numpy as jnp
from jax import lax
from jax.experimental import pallas as pl
from jax.experimental.pallas import tpu as pltpu

N_HEADS = 16
HEAD_DIM = 64
POOL_WINDOWS = (2, 4, 8, 16)
MAX_WINDOW = max(POOL_WINDOWS)
Q_BLOCK = 128
RMS_EPS = 1e-6
FFN_RES_WEIGHT = 0.5

LANES = 128
BF16_SUBLANES = 16
MXU_DIM = 256
VMEM_LIMIT_BYTES = 56 * 1024 * 1024

FFN_ROWS = 1024
FFN_CHUNK = MXU_DIM
POOL_ROW_BLOCKS = 2
PROJ_ROWS = 1024
HEADS_PER_STEP = LANES // HEAD_DIM
ATTN_Q_BLOCKS = 16
ATTN_WAVES = 2
ATTN_UNROLLED_VISITS = 3
ATTN_LAST_VISIT_ROWS = 32

DEAD_LOG = 105.0
LOG2_E = 1.4426950408889634

_BF16 = jnp.bfloat16
_F32 = jnp.float32


def _resident(shape):
    zeros = (0,) * len(shape)
    return pl.BlockSpec(shape, lambda *_: zeros, pipeline_mode=pl.Buffered(1))


def _rms_norm(x, gain):
    ms = jnp.mean(x * x, axis=-1, keepdims=True)
    return x * lax.rsqrt(ms + RMS_EPS) * gain


def _dot(a, b):
    return jnp.dot(a, b, preferred_element_type=_F32)


def _ffn_update(x, g_ref, win_ref, wout_ref, o_ref, side_work=()):
    d_ff = wout_ref.shape[0]
    n_chunks = d_ff // FFN_CHUNK
    emit_after = {(k + 1) * n_chunks // (len(side_work) + 1) - 1: work
                  for k, work in enumerate(side_work)}
    xn = _rms_norm(x, g_ref[...]).astype(_BF16)
    acc = x * (1.0 / FFN_RES_WEIGHT)
    for c in range(n_chunks):
        lo = c * FFN_CHUNK
        gate = _dot(xn, win_ref[:, lo:lo + FFN_CHUNK])
        up = _dot(xn, win_ref[:, d_ff + lo:d_ff + lo + FFN_CHUNK])
        act = (gate * jax.nn.sigmoid(gate) * up).astype(_BF16)
        acc = acc + _dot(act, wout_ref[lo:lo + FFN_CHUNK, :])
        if c in emit_after:
            emit_after[c]()
    o_ref[...] = FFN_RES_WEIGHT * acc


def _pool_pieces(x_ref, halo_ref, first_of_seq, g_ref, w_ref, scale_ref, o_ref):
    group_w = w_ref.shape[1]
    block = x_ref.shape[0] // POOL_ROW_BLOCKS
    shared = {}

    def normed(rb):
        if rb not in shared:
            gain = g_ref[...]
            if rb == 0:
                before = jnp.where(first_of_seq, 0.0, _rms_norm(halo_ref[...], gain))
                shared[rb] = jnp.concatenate([before, _rms_norm(x_ref[:block], gain)], axis=0)
            else:
                shared[rb] = _rms_norm(x_ref[rb * block - MAX_WINDOW:(rb + 1) * block], gain)
        return shared[rb]

    def piece(rb, g, w):
        ext = normed(rb)
        cols = slice(g * group_w, (g + 1) * group_w)
        rows = slice(rb * block, (rb + 1) * block)
        s = ext[:, cols]
        shift = 1
        while shift < w:
            s = s + pltpu.roll(s, shift, axis=0)
            shift *= 2
        if rb == 0:
            row = lax.broadcasted_iota(jnp.int32, (block, 1), 0)
            cnt = jnp.where(first_of_seq, jnp.minimum(row + 1, w), w)
            mean = s[MAX_WINDOW:] * (1.0 / cnt.astype(_F32))
        else:
            mean = s[MAX_WINDOW:] * (1.0 / w)
        y = _dot((mean - ext[MAX_WINDOW:, cols]).astype(_BF16), w_ref[g])
        o_ref[rows, cols] = x_ref[rows, cols] + y * scale_ref[:, cols]

    return [functools.partial(piece, rb, g, w)
            for rb in range(POOL_ROW_BLOCKS) for g, w in enumerate(POOL_WINDOWS)]


def _ffn_kernel(x_ref, g_ref, win_ref, wout_ref, o_ref):
    _ffn_update(x_ref[...], g_ref, win_ref, wout_ref, o_ref)


def _proj_ffn_kernel(x_ref, a_ref, wp_ref, g_ref, win_ref, wout_ref, o_ref):
    a = jnp.concatenate([a_ref[c] for c in range(a_ref.shape[0])], axis=1)
    x = x_ref[...] + _dot(a, wp_ref[...])
    _ffn_update(x, g_ref, win_ref, wout_ref, o_ref)


def _pool_ffn_kernel(x_ref, halo_ref, pg_ref, pw_ref, ps_ref, g_ref, win_ref, wout_ref, o_ref,
                     pooled_ref, *, tiles_per_seq):
    i = pl.program_id(0)
    tile = jnp.minimum(i, pl.num_programs(0) - 2)
    first_of_seq = tile % tiles_per_seq == 0

    def pool_pieces():
        return _pool_pieces(x_ref, halo_ref, first_of_seq, pg_ref, pw_ref, ps_ref, pooled_ref)

    @pl.when(i == 0)
    def _():
        for piece in pool_pieces():
            piece()

    @pl.when(i > 0)
    def _():
        _ffn_update(pooled_ref[...], g_ref, win_ref, wout_ref, o_ref, side_work=pool_pieces())


def _layer_resident(stacked, layer):
    index = (layer,) + (0,) * (stacked.ndim - 1)
    return pl.BlockSpec((None,) + stacked.shape[1:], lambda *_: index,
                        pipeline_mode=pl.Buffered(1))


def _ffn_call(body, name, t, d, steps, row_maps, specs, args, gain, w_in, w_out, layer,
              scratch=()):
    d_ff = w_out.shape[1]
    assert t % FFN_ROWS == 0 and d_ff % FFN_CHUNK == 0 and w_in.shape[1:] == (d, 2 * d_ff)
    assert w_in.dtype == _BF16 and w_out.dtype == _BF16
    ffn_specs = [_resident((1, d)), _layer_resident(w_in, layer), _layer_resident(w_out, layer)]
    return pl.pallas_call(
        body,
        grid=(steps,),
        in_specs=specs + ffn_specs,
        out_specs=pl.BlockSpec((FFN_ROWS, d), row_maps),
        out_shape=jax.ShapeDtypeStruct((t, d), _F32),
        scratch_shapes=list(scratch),
        compiler_params=pltpu.CompilerParams(
            dimension_semantics=("arbitrary",), vmem_limit_bytes=VMEM_LIMIT_BYTES),
        name=name,
    )(*args, gain.reshape(1, d), w_in, w_out)


def _ffn(h, gain, w_in, w_out, layer):
    t, d = h.shape
    rows = lambda i: (i, 0)
    return _ffn_call(_ffn_kernel, "ffn", t, d, t // FFN_ROWS, rows,
                     [pl.BlockSpec((FFN_ROWS, d), rows)], (h,), gain, w_in, w_out, layer)


def _proj_ffn(h, mixer_out, w_proj, gain, w_in, w_out, layer):
    t, d = h.shape
    blocks = mixer_out.shape[0]
    assert mixer_out.shape == (blocks, t, LANES) and w_proj.shape == (blocks * LANES, d)
    rows = lambda i: (i, 0)
    specs = [pl.BlockSpec((FFN_ROWS, d), rows),
             pl.BlockSpec((blocks, FFN_ROWS, LANES), lambda i: (0, i, 0)),
             _resident(w_proj.shape)]
    return _ffn_call(_proj_ffn_kernel, "proj_ffn", t, d, t // FFN_ROWS, rows, specs,
                     (h, mixer_out, w_proj.astype(_BF16)), gain, w_in, w_out, layer)


def _pool_ffn(h, seq_len, pool_gain, pool_w, pool_scale, gain, w_in, w_out, layer):
    t, d = h.shape
    assert seq_len % FFN_ROWS == 0 and FFN_ROWS % MAX_WINDOW == 0 and t % seq_len == 0
    tiles = t // FFN_ROWS
    halo_blocks = FFN_ROWS // MAX_WINDOW
    pooled_tile = lambda i: jnp.minimum(i, tiles - 1)
    specs = [pl.BlockSpec((FFN_ROWS, d), lambda i: (pooled_tile(i), 0)),
             pl.BlockSpec((MAX_WINDOW, d),
                          lambda i: (jnp.maximum(pooled_tile(i) * halo_blocks - 1, 0), 0)),
             _resident((1, d)), _resident(pool_w.shape), _resident((1, d))]
    args = (h, h, pool_gain.reshape(1, d), pool_w.astype(_BF16), pool_scale.reshape(1, d))
    return _ffn_call(
        functools.partial(_pool_ffn_kernel, tiles_per_seq=seq_len // FFN_ROWS), "pool_ffn",
        t, d, tiles + 1, lambda i: (jnp.maximum(i - 1, 0), 0), specs, args, gain, w_in, w_out,
        layer, scratch=[pltpu.VMEM((FFN_ROWS, d), _F32)])


def _head_proj_kernel(x_ref, g_ref, w_ref, hg_ref, *o_refs, d_norm):
    xn = _rms_norm(x_ref[...], g_ref[...]).astype(_BF16)
    y = _dot(xn, w_ref[...])
    first_head = lax.broadcasted_iota(jnp.int32, (y.shape[0], LANES), 1) < HEAD_DIM
    for c in range(d_norm // LANES):
        cols = slice(c * LANES, (c + 1) * LANES)
        yc = y[:, cols]
        sq = yc * yc
        ms = [jnp.sum(jnp.where(keep, sq, 0.0), axis=-1, keepdims=True) * (1.0 / HEAD_DIM)
              for keep in (first_head, ~first_head)]
        r = jnp.where(first_head, lax.rsqrt(ms[0] + RMS_EPS), lax.rsqrt(ms[1] + RMS_EPS))
        o_refs[0][c] = (yc * r * hg_ref[:, cols]).astype(o_refs[0].dtype)
    if len(o_refs) > 1:
        for c in range(o_refs[1].shape[0]):
            o_refs[1][c] = y[:, d_norm + c * LANES:d_norm + (c + 1) * LANES].astype(
                o_refs[1].dtype)


def _head_proj(h, gain, w, head_gain, d_norm):
    t, d = h.shape
    d_out = w.shape[1]
    assert t % PROJ_ROWS == 0 and HEADS_PER_STEP == 2
    assert d_norm % LANES == 0 and (d_out - d_norm) % LANES == 0
    hg = jnp.tile(head_gain, d_norm // HEAD_DIM).reshape(1, d_norm)
    row_spec = pl.BlockSpec((PROJ_ROWS, d), lambda i: (i, 0))
    widths = [d_norm] + ([d_out - d_norm] if d_out > d_norm else [])
    out_shapes = [jax.ShapeDtypeStruct((n // LANES, t, LANES), _BF16) for n in widths]
    out_specs = [pl.BlockSpec((n // LANES, PROJ_ROWS, LANES), lambda i: (0, i, 0)) for n in widths]
    return pl.pallas_call(
        functools.partial(_head_proj_kernel, d_norm=d_norm),
        grid=(t // PROJ_ROWS,),
        in_specs=[row_spec, _resident((1, d)), _resident(w.shape), _resident((1, d_norm))],
        out_specs=out_specs,
        out_shape=out_shapes,
        compiler_params=pltpu.CompilerParams(
            dimension_semantics=("arbitrary",), vmem_limit_bytes=VMEM_LIMIT_BYTES),
        name="head_proj",
    )(h, gain.reshape(1, d), w.astype(_BF16), hg)


def _attn_kernel(q_ref, k_ref, v_ref, o_ref, acc_ref, dead_ref):
    ig = pl.program_id(2)

    src = lax.broadcasted_iota(jnp.int32, (Q_BLOCK, Q_BLOCK), 0)
    dst = lax.broadcasted_iota(jnp.int32, (Q_BLOCK, Q_BLOCK), 1)
    suffix_mat = (src > dst).astype(_BF16)

    t_idx = lax.broadcasted_iota(jnp.int32, (2 * Q_BLOCK, Q_BLOCK), 0) % Q_BLOCK
    s_idx = lax.broadcasted_iota(jnp.int32, (2 * Q_BLOCK, Q_BLOCK), 1)
    causal = s_idx < t_idx

    def stack_heads(x):
        zero = jnp.zeros_like(x)
        keep = lax.broadcasted_iota(jnp.int32, x.shape, 1) < HEAD_DIM
        return jnp.concatenate([jnp.where(keep, x, zero), jnp.where(keep, zero, x)], axis=0)

    def load_kv(j):
        start = j * Q_BLOCK
        if not isinstance(start, int):
            start = pl.multiple_of(start, Q_BLOCK)
        rows = pl.ds(start, Q_BLOCK)
        return k_ref[rows, :], stack_heads(v_ref[rows, :])

    def score(q2, kj):
        return lax.dot_general(q2, kj, (((1,), (1,)), ((), ())), preferred_element_type=_F32)

    def log_terms(z, mask):
        y = jnp.maximum(z, 0.0) + jnp.log(1.0 + jnp.exp2(jnp.abs(z) * (-LOG2_E)))
        return (y if mask is None else jnp.where(mask, y, 0.0)), z - y

    def later_sums(y):
        return _dot(y.astype(_BF16), suffix_mat)

    def weights(log_sig, later, dead, mask):
        drop = later if dead is None else later + dead
        a = jnp.exp(log_sig - drop)
        if mask is not None:
            a = jnp.where(mask, a, 0.0)
        a = a.astype(_BF16)
        rows = a.shape[0] // HEADS_PER_STEP
        return jnp.concatenate([a[:rows], a[rows:]], axis=1)

    def first_rows(x, rows):
        return x if rows == Q_BLOCK else jnp.concatenate(
            [x[:rows], x[Q_BLOCK:Q_BLOCK + rows]], axis=0)

    def add_first_rows(x, part, rows):
        if rows == Q_BLOCK:
            return x + part
        pieces = []
        for h in range(x.shape[0] // Q_BLOCK):
            pieces += [x[h * Q_BLOCK:h * Q_BLOCK + rows] + part[h * rows:(h + 1) * rows],
                       x[h * Q_BLOCK + rows:(h + 1) * Q_BLOCK]]
        return jnp.concatenate(pieces, axis=0)

    def visit_rows(back):
        return ATTN_LAST_VISIT_ROWS if back == ATTN_UNROLLED_VISITS - 1 else Q_BLOCK

    def rows_of(parts):
        return parts[0] if len(parts) == 1 else jnp.concatenate(parts, axis=0)

    def split_rows(x, visits, rows_per_query):
        out, at = {}, 0
        for v in visits:
            n = visit_rows(v[1]) * rows_per_query
            out[v] = x[at:at + n]
            at += n
        return out

    def unrolled_visits(first_block):
        per_wave = ATTN_Q_BLOCKS // ATTN_WAVES
        kv_cache = {}
        alive = None
        for w in range(ATTN_WAVES):
            wave_min = unrolled_wave(first_block, range(w * per_wave, (w + 1) * per_wave), kv_cache)
            alive = wave_min if alive is None else jnp.minimum(alive, wave_min)
        return alive < DEAD_LOG

    def unrolled_wave(first_block, blocks, kv):
        users = {}
        for g in blocks:
            for back in range(ATTN_UNROLLED_VISITS):
                if isinstance(first_block, int) and first_block + g - back < 0:
                    continue
                users.setdefault(g - back, []).append((g, back))
        offsets = sorted(users, reverse=True)
        order = [v for e in offsets for v in users[e]]
        q = {g: q_ref[g * Q_BLOCK:(g + 1) * Q_BLOCK, :] for g in blocks}
        q2 = {v: stack_heads(q[v[0]][:visit_rows(v[1])]) for v in order}
        for e in offsets:
            if e not in kv:
                kv[e] = load_kv(first_block + e)

        z = {}
        for e in offsets:
            zc = score(rows_of([q2[v] for v in users[e]]), kv[e][0])
            z.update(split_rows(zc, users[e], HEADS_PER_STEP))
        y, log_sig = {}, {}
        for v in order:
            y[v], log_sig[v] = log_terms(z[v], causal if v[1] == 0 else None)
        later = split_rows(later_sums(rows_of([y[v] for v in order])), order, HEADS_PER_STEP)
        total = {v: jnp.sum(y[v], axis=-1, keepdims=True) for v in order}

        a_wide, dead_end = {}, {}
        for g in blocks:
            dead = None
            for back in range(ATTN_UNROLLED_VISITS):
                v = (g, back)
                if v in z:
                    rows = visit_rows(back)
                    a_wide[v] = weights(log_sig[v], later[v],
                                        None if dead is None else first_rows(dead, rows),
                                        causal if back == 0 else None)
                    dead = total[v] if dead is None else add_first_rows(dead, total[v], rows)
            dead_end[g] = dead

        acc = {}
        for e in offsets:
            c = split_rows(_dot(rows_of([a_wide[v] for v in users[e]]), kv[e][1]), users[e], 1)
            for v in users[e]:
                g = v[0]
                acc[g] = c[v] if g not in acc else add_first_rows(acc[g], c[v], visit_rows(v[1]))

        most_alive = None
        for g in blocks:
            acc_ref[g] = acc[g]
            dead_ref[g] = dead_end[g]
            g_min = jnp.min(dead_end[g])
            most_alive = g_min if most_alive is None else jnp.minimum(most_alive, g_min)
        return most_alive

    def visit(q2, kv, dead, mask):
        y, log_sig = log_terms(score(q2, kv[0]), mask)
        a = weights(log_sig, later_sums(y), dead, mask)
        return _dot(a, kv[1]), dead + jnp.sum(y, axis=-1, keepdims=True)

    def remaining_visits(first_block):
        for g in range(ATTN_Q_BLOCKS):
            last_block = first_block + g - (ATTN_UNROLLED_VISITS - 1)
            if isinstance(last_block, int) and last_block < 0:
                continue
            q2 = stack_heads(q_ref[g * Q_BLOCK:(g + 1) * Q_BLOCK, :])

            def step(j, mask, g=g, q2=q2):
                contrib, dead = visit(q2, load_kv(j), dead_ref[g], mask)
                acc_ref[g] += contrib
                dead_ref[g] = dead
                return jnp.min(dead) < DEAD_LOG

            def cond(carry):
                j, alive = carry
                return jnp.logical_and(j >= 0, alive)

            def body(carry, step=step):
                j, _ = carry
                return j - 1, step(j, None)

            @pl.when(jnp.min(dead_ref[g]) < DEAD_LOG)
            def _(last_block=last_block, step=step, cond=cond, body=body):
                alive = step(last_block, t_idx >= ATTN_LAST_VISIT_ROWS)
                lax.while_loop(cond, body, (last_block - 1, alive))

    @pl.when(ig == 0)
    def _():
        pl.when(unrolled_visits(0))(lambda: remaining_visits(0))

    @pl.when(ig > 0)
    def _():
        first_block = ig * ATTN_Q_BLOCKS
        pl.when(unrolled_visits(first_block))(lambda: remaining_visits(first_block))

    for g in range(ATTN_Q_BLOCKS):
        o_ref[g * Q_BLOCK:(g + 1) * Q_BLOCK, :] = acc_ref[g].astype(o_ref.dtype)


def _attention(q, k, v):
    blocks, b, s, _ = q.shape
    group_rows = ATTN_Q_BLOCKS * Q_BLOCK
    assert ATTN_Q_BLOCKS >= ATTN_UNROLLED_VISITS - 1
    assert s % group_rows == 0 and q.shape[-1] == LANES
    assert 0 < ATTN_LAST_VISIT_ROWS <= Q_BLOCK and ATTN_LAST_VISIT_ROWS % BF16_SUBLANES == 0
    q_spec = pl.BlockSpec((None, None, group_rows, LANES), lambda bi, c, i: (c, bi, i, 0))
    kv_spec = pl.BlockSpec((None, None, s, LANES), lambda bi, c, i: (c, bi, 0, 0))
    return pl.pallas_call(
        _attn_kernel,
        grid=(b, blocks, s // group_rows),
        in_specs=[q_spec, kv_spec, kv_spec],
        out_specs=q_spec,
        out_shape=jax.ShapeDtypeStruct(q.shape, _BF16),
        scratch_shapes=[pltpu.VMEM((ATTN_Q_BLOCKS, Q_BLOCK, LANES), _F32),
                        pltpu.VMEM((ATTN_Q_BLOCKS, HEADS_PER_STEP * Q_BLOCK, 1), _F32)],
        compiler_params=pltpu.CompilerParams(
            dimension_semantics=("arbitrary", "arbitrary", "arbitrary"),
            vmem_limit_bytes=VMEM_LIMIT_BYTES),
        name="attn",
    )(q, k, v)


def kernel(x, ffn1_norm, ffn1_w_in, ffn1_w_out, ffn2_norm, ffn2_w_in, ffn2_w_out, pool_norm,
           pool_w, pool_scale, kv_norm, w_kv, k_gain, attn_norm, w_q, q_gain, w_o):
    b, s, d = x.shape
    assert d == N_HEADS * HEAD_DIM
    h = x.reshape(b * s, d)
    w1_in, w1_out, w2_in, w2_out = (
        w.astype(_BF16) for w in (ffn1_w_in, ffn1_w_out, ffn2_w_in, ffn2_w_out))

    h = _ffn(h, ffn1_norm[0], w1_in, w1_out, layer=0)
    h = _pool_ffn(h, s, pool_norm[0], pool_w[0], pool_scale[0], ffn2_norm[0], w2_in, w2_out,
                  layer=0)

    k, v = _head_proj(h, kv_norm, w_kv, k_gain, d_norm=d)

    h = _ffn(h, ffn1_norm[1], w1_in, w1_out, layer=1)
    (q,) = _head_proj(h, attn_norm[0], w_q[0], q_gain[0] * (HEAD_DIM ** -0.5), d_norm=d)
    per_batch = lambda a: a.reshape(a.shape[0], b, s, LANES)
    o = _attention(per_batch(q), per_batch(k), per_batch(v))
    h = _proj_ffn(h, o.reshape(o.shape[0], b * s, LANES), w_o[0], ffn2_norm[1], w2_in, w2_out,
                  layer=1)
    return h.reshape(b, s, d)
```
